```python
import jax, jax.numpy as jnp
from jax import lax
import numpy as np

D_MODEL = 4096
BATCH = 2
SEQ = 4096
DEPTH = 4

GRID_W = 64
CTX_LEN = 256
N_MIXERS = 2
N_RG = (DEPTH + N_MIXERS - 1) // N_MIXERS
N_NA = DEPTH // N_MIXERS
ADA_GAIN = 0.5
D_RNN = D_MODEL
LRU_BLOCK = 256
LRU_N_BLOCKS = D_RNN // LRU_BLOCK
CONV_W = 4
LRU_C = 8.0
N_HEADS = 32
HEAD_DIM = D_MODEL // N_HEADS
NA_KH = 8
NA_KW = 16
N_COL_BLOCKS = GRID_W // NA_KW
BAND_W = 2 * NA_KW
ATTN_SCALE = HEAD_DIM ** -0.5
EPS = 1e-6

kernel_name = "hybrid_rglru_natten_prefix_dit"


def rms_norm(x, g):
    xf = x.astype(jnp.float32)
    y = xf * lax.rsqrt(jnp.mean(xf * xf, axis=-1, keepdims=True) + EPS)
    return (y * g.astype(jnp.float32)).astype(x.dtype)


def centred_conv(u, w, b):
    t = u.shape[1]
    left = CONV_W // 2
    up = jnp.pad(u, ((0, 0), (left, CONV_W - 1 - left), (0, 0)))
    y = w[0] * up[:, 0:t]
    for k in range(1, CONV_W):
        y = y + w[k] * up[:, k:k + t]
    return y + b


def _affine_combine(left, right):
    a_l, b_l = left
    a_r, b_r = right
    return a_l * a_r, a_r * b_l + b_r


def linear_scan(a, b, h0, reverse):
    if reverse:
        a, b = a[:, ::-1], b[:, ::-1]
    b = b.at[:, 0].add(a[:, 0] * h0)
    _, hs = lax.associative_scan(_affine_combine, (a, b), axis=1)
    return hs[:, ::-1] if reverse else hs


def lru_coeffs(u, gate_w, gate_b, lam):
    bsz, t, _ = u.shape
    ub = u.reshape(bsz, t, LRU_N_BLOCKS, LRU_BLOCK)
    z = jnp.einsum('btnk,gnkj->gbtnj', ub, gate_w.astype(jnp.float32)).reshape(2, bsz, t, D_RNN)
    z = z + gate_b.astype(jnp.float32)[:, None, None, :]
    r = jax.nn.sigmoid(z[0])
    i_gate = jax.nn.sigmoid(z[1])
    log_a = -LRU_C * r * jax.nn.softplus(-lam.astype(jnp.float32))
    a = jnp.exp(log_a)
    b = jnp.sqrt(-jnp.expm1(2.0 * log_a)) * (i_gate * u)
    return a, b


def rglru_mixer(h, hc, in_w, conv_w, conv_b, gate_w, gate_b, lam, out_w, need_ctx):
    f32 = jnp.float32
    u, g = jnp.split(h @ in_w, 2, axis=-1)
    if need_ctx:
        uc, gc = jnp.split(hc @ in_w, 2, axis=-1)
    else:
        uc = hc @ in_w[:, :D_RNN]
    u = centred_conv(u, conv_w, conv_b).astype(f32)
    uc = centred_conv(uc, conv_w, conv_b).astype(f32)
    zeros = jnp.zeros((uc.shape[0], D_RNN), f32)
    a, b = lru_coeffs(uc, gate_w[0], gate_b[0], lam[0])
    hc_f = linear_scan(a, b, zeros, reverse=False)
    a, b = lru_coeffs(u, gate_w[0], gate_b[0], lam[0])
    h_f = linear_scan(a, b, hc_f[:, -1], reverse=False)
    a, b = lru_coeffs(uc, gate_w[1], gate_b[1], lam[1])
    hc_b = linear_scan(a, b, zeros, reverse=True)
    a, b = lru_coeffs(u, gate_w[1], gate_b[1], lam[1])
    h_b = linear_scan(a, b, hc_b[:, 0], reverse=True)
    y = ((h_f + h_b).astype(h.dtype) * jax.nn.silu(g)) @ out_w
    yc = (((hc_f + hc_b).astype(hc.dtype) * jax.nn.silu(gc)) @ out_w) if need_ctx else None
    return y, yc


def neighbourhood_attention(q, k, v, kc, vc, rpb):
    bsz, seq, n_heads, dh = q.shape
    rows = seq // GRID_W
    kh = min(NA_KH, rows)
    qg = q.reshape(bsz, rows, N_COL_BLOCKS, NA_KW, n_heads, dh)
    kg = k.reshape(bsz, rows, GRID_W, n_heads, dh)
    vg = v.reshape(bsz, rows, GRID_W, n_heads, dh)
    q_cols = jnp.arange(GRID_W).reshape(N_COL_BLOCKS, NA_KW)
    q_col_start = jnp.clip(q_cols - NA_KW // 2, 0, GRID_W - NA_KW)
    band_start = jnp.clip(jnp.arange(N_COL_BLOCKS) * NA_KW - NA_KW // 2, 0, GRID_W - BAND_W)
    band_cols = band_start[:, None] + jnp.arange(BAND_W)
    kcol = band_cols[:, None, :]
    col_mask = (kcol >= q_col_start[..., None]) & (kcol < q_col_start[..., None] + NA_KW)
    col_idx = jnp.clip(kcol - q_cols[..., None], -(NA_KW - 1), NA_KW - 1) + (NA_KW - 1)
    rpb_cols = rpb[:, :, col_idx]
    n_lat = kh * BAND_W

    def one_row(r):
        row_start = jnp.clip(r - kh // 2, 0, rows - kh)
        kr = jnp.take(lax.dynamic_slice_in_dim(kg, row_start, kh, axis=1), band_cols, axis=2)
        vr = jnp.take(lax.dynamic_slice_in_dim(vg, row_start, kh, axis=1), band_cols, axis=2)
        qr = lax.dynamic_index_in_dim(qg, r, axis=1, keepdims=False)
        s_lat = jnp.einsum('bnqhd,binchd->bhnqic', qr, kr, preferred_element_type=jnp.float32) * ATTN_SCALE
        row_idx = row_start + jnp.arange(kh) - r + (NA_KH - 1)
        bias = jnp.take(rpb_cols, row_idx, axis=1).transpose(0, 2, 3, 1, 4)
        s_lat = jnp.where(col_mask[:, :, None, :], s_lat + bias.astype(jnp.float32), -jnp.inf)
        s_ctx = jnp.einsum('bnqhd,bkhd->bhnqk', qr, kc, preferred_element_type=jnp.float32) * ATTN_SCALE
        sh = s_lat.shape
        p = jax.nn.softmax(jnp.concatenate([s_lat.reshape(sh[0], sh[1], sh[2], sh[3], n_lat), s_ctx], axis=-1), axis=-1)
        p_lat = p[..., :n_lat].reshape(sh).astype(v.dtype)
        p_ctx = p[..., n_lat:].astype(v.dtype)
        return (jnp.einsum('bhnqic,binchd->bnqhd', p_lat, vr)
                + jnp.einsum('bhnqk,bkhd->bnqhd', p_ctx, vc))

    out = lax.map(one_row, jnp.arange(rows))
    return jnp.moveaxis(out, 0, 1).reshape(bsz, seq, n_heads, dh)


def context_attention(qc, kc, vc):
    s = jnp.einsum('bqhd,bkhd->bhqk', qc, kc, preferred_element_type=jnp.float32) * ATTN_SCALE
    p = jax.nn.softmax(s, axis=-1).astype(vc.dtype)
    return jnp.einsum('bhqk,bkhd->bqhd', p, vc)


def na_mixer(h, hc, in_w, qk_norm, rpb, out_w, need_ctx):
    bsz, seq, _ = h.shape
    n_ctx = hc.shape[1]
    q, k, v, g = jnp.split(h @ in_w, 4, axis=-1)
    q = rms_norm(q.reshape(bsz, seq, N_HEADS, HEAD_DIM), qk_norm[0])
    k = rms_norm(k.reshape(bsz, seq, N_HEADS, HEAD_DIM), qk_norm[1])
    v = v.reshape(bsz, seq, N_HEADS, HEAD_DIM)
    if need_ctx:
        qc, kc, vc, gc = jnp.split(hc @ in_w, 4, axis=-1)
    else:
        kc, vc = jnp.split(hc @ in_w[:, D_MODEL:3 * D_MODEL], 2, axis=-1)
    kc = rms_norm(kc.reshape(bsz, n_ctx, N_HEADS, HEAD_DIM), qk_norm[1])
    vc = vc.reshape(bsz, n_ctx, N_HEADS, HEAD_DIM)
    o = neighbourhood_attention(q, k, v, kc, vc, rpb)
    y = (o.reshape(bsz, seq, D_MODEL) * jax.nn.silu(g)) @ out_w
    yc = None
    if need_ctx:
        qc = rms_norm(qc.reshape(bsz, n_ctx, N_HEADS, HEAD_DIM), qk_norm[0])
        oc = context_attention(qc, kc, vc)
        yc = (oc.reshape(bsz, n_ctx, D_MODEL) * jax.nn.silu(gc)) @ out_w
    return y, yc


def setup_inputs(seed: int = 0) -> dict:
    key = jax.random.key(seed)
    ks = jax.random.split(key, 18)
    f32 = jnp.float32

    def nrm(k, shape, s):
        return jax.random.normal(k, shape, f32) * s

    a_pow = jax.random.uniform(ks[12], (N_RG, 2, D_RNN), f32, 0.9, 0.999)
    s_lam = a_pow ** (1.0 / LRU_C)
    return {
        'x': nrm(ks[0], (BATCH, SEQ, D_MODEL), 1.0),
        'c': nrm(ks[1], (BATCH, D_MODEL), 1.0),
        'ctx': nrm(ks[2], (BATCH, CTX_LEN, D_MODEL), 1.0),
        'c_ctx': nrm(ks[3], (D_MODEL,), 1.0),
        'ada_w': nrm(ks[4], (DEPTH, D_MODEL, 3 * D_MODEL), ADA_GAIN * D_MODEL ** -0.5),
        'ada_b': nrm(ks[5], (DEPTH, 3 * D_MODEL), 0.01),
        'norm_g': 1.0 + nrm(ks[6], (DEPTH, D_MODEL), 0.02),
        'lru_in_w': nrm(ks[7], (N_RG, D_MODEL, 2 * D_RNN), D_MODEL ** -0.5),
        'lru_conv_w': nrm(ks[8], (N_RG, CONV_W, D_RNN), CONV_W ** -0.5),
        'lru_conv_b': nrm(ks[9], (N_RG, D_RNN), 0.01),
        'lru_gate_w': nrm(ks[10], (N_RG, 2, 2, LRU_N_BLOCKS, LRU_BLOCK, LRU_BLOCK), LRU_BLOCK ** -0.5),
        'lru_gate_b': nrm(ks[11], (N_RG, 2, 2, D_RNN), 0.01),
        'lru_lambda': jnp.log(s_lam) - jnp.log1p(-s_lam),
        'lru_out_w': nrm(ks[13], (N_RG, D_RNN, D_MODEL), D_RNN ** -0.5),
        'na_in_w': nrm(ks[14], (N_NA, D_MODEL, 4 * D_MODEL), D_MODEL ** -0.5),
        'na_qk_norm': 1.0 + nrm(ks[15], (N_NA, 2, HEAD_DIM), 0.02),
        'na_rpb': nrm(ks[16], (N_NA, N_HEADS, 2 * NA_KH - 1, 2 * NA_KW - 1), 0.1),
        'na_out_w': nrm(ks[17], (N_NA, D_MODEL, D_MODEL), D_MODEL ** -0.5),
    }


def reference(x, c, ctx, c_ctx, ada_w, ada_b, norm_g, lru_in_w, lru_conv_w, lru_conv_b, lru_gate_w,
              lru_gate_b, lru_lambda, lru_out_w, na_in_w, na_qk_norm, na_rpb, na_out_w):
    cond = jax.nn.silu(c)
    cond_ctx = jax.nn.silu(c_ctx)
    for i in range(DEPTH):
        need_ctx = i < DEPTH - 1
        shift, scale, gate = jnp.split(cond @ ada_w[i] + ada_b[i], 3, axis=-1)
        shift_c, scale_c, gate_c = jnp.split(cond_ctx @ ada_w[i] + ada_b[i], 3, axis=-1)
        h = rms_norm(x, norm_g[i]) * (1.0 + scale[:, None, :]) + shift[:, None, :]
        hc = rms_norm(ctx, norm_g[i]) * (1.0 + scale_c) + shift_c
        j = i // N_MIXERS
        if i % N_MIXERS == 0:
            y, yc = rglru_mixer(h, hc, lru_in_w[j], lru_conv_w[j], lru_conv_b[j], lru_gate_w[j],
                                lru_gate_b[j], lru_lambda[j], lru_out_w[j], need_ctx)
        else:
            y, yc = na_mixer(h, hc, na_in_w[j], na_qk_norm[j], na_rpb[j], na_out_w[j], need_ctx)
        x = x + gate[:, None, :] * y
        if need_ctx:
            ctx = ctx + gate_c * yc
    return x
```

```python
import functools

import jax
import jax.numpy as jnp
from jax import lax
from jax.experimental import pallas as pl
from jax.experimental.pallas import tpu as pltpu

GRID_W = 64
NA_KH = 8
NA_KW = 16
HEAD_DIM = 128
LRU_BLOCK = 256
CONV_W = 4
LRU_C = 8.0
EPS = 1e-6
ATTN_SCALE = HEAD_DIM ** -0.5
MASK_VALUE = -1e30

V7X_VMEM_BYTES = 64 * 1024 * 1024
SUBLANES = 8
LANES = 128
MOD_ROWS = 8

F32 = jnp.float32
BF16 = jnp.bfloat16


def _vmem_limit(block_bytes):
    return int(min(block_bytes + (8 << 20), V7X_VMEM_BYTES - (4 << 20)))


def _silu(v):
    return v * jax.nn.sigmoid(v)


def _adaln_kernel(c_ref, w_ref, b_ref, o_ref):
    cond = _silu(c_ref[...]).astype(BF16)
    w = w_ref[0].astype(BF16)
    o_ref[0] = jnp.dot(cond, w, preferred_element_type=F32) + b_ref[0]


def _adaln(cvec, ada_w, ada_b):
    n_layers, d, n3 = ada_w.shape
    tn = min(512, n3)
    block_bytes = 2 * d * tn * 4 + d * tn * 2 + 4 * MOD_ROWS * (d + 2 * tn) * 4
    return pl.pallas_call(
        _adaln_kernel,
        out_shape=jax.ShapeDtypeStruct((n_layers, MOD_ROWS, n3), F32),
        grid=(n_layers, n3 // tn),
        in_specs=[
            pl.BlockSpec((MOD_ROWS, d), lambda l, j: (0, 0)),
            pl.BlockSpec((1, d, tn), lambda l, j: (l, 0, j)),
            pl.BlockSpec((1, 1, tn), lambda l, j: (l, 0, j)),
        ],
        out_specs=pl.BlockSpec((1, MOD_ROWS, tn), lambda l, j: (l, 0, j)),
        compiler_params=pltpu.CompilerParams(
            dimension_semantics=("parallel", "parallel"), vmem_limit_bytes=_vmem_limit(block_bytes)),
        name="adaln",
    )(cvec, ada_w, ada_b.reshape(n_layers, 1, n3))


def _norm_kernel(x_ref, g_ref, mod_ref, o_ref, *, tiles_per_group, group0, d):
    grp = group0 + pl.program_id(0) // tiles_per_group
    x = x_ref[...]
    ms = jnp.mean(x * x, axis=-1, keepdims=True)
    y = x * lax.rsqrt(ms + EPS) * g_ref[...]
    shift = mod_ref[pl.ds(grp, 1), 0:d]
    scale = mod_ref[pl.ds(grp, 1), d:2 * d]
    o_ref[...] = (y * (1.0 + scale) + shift).astype(BF16)


def _norm_mod(x2, g, mod_l, rows_per_group, group0):
    m, d = x2.shape
    tm = min(256, rows_per_group)
    block_bytes = 2 * tm * d * 4 + 2 * tm * d * 2 + 2 * MOD_ROWS * 3 * d * 4 + 3 * tm * d * 4
    kern = functools.partial(_norm_kernel, tiles_per_group=rows_per_group // tm, group0=group0, d=d)
    return pl.pallas_call(
        kern,
        out_shape=jax.ShapeDtypeStruct((m, d), BF16),
        grid=(m // tm,),
        in_specs=[
            pl.BlockSpec((tm, d), lambda i: (i, 0)),
            pl.BlockSpec((1, d), lambda i: (0, 0)),
            pl.BlockSpec((MOD_ROWS, 3 * d), lambda i: (0, 0)),
        ],
        out_specs=pl.BlockSpec((tm, d), lambda i: (i, 0)),
        compiler_params=pltpu.CompilerParams(
            dimension_semantics=("parallel",), vmem_limit_bytes=_vmem_limit(block_bytes)),
        name="norm_mod",
    )(x2, g.reshape(1, d), mod_l)


def _mm_tiles(m, n):
    tm = min(1024, m)
    tn = min(1024, n)
    return tm, tn


def _mm_plain_kernel(a_ref, w_ref, o_ref):
    o_ref[...] = jnp.dot(a_ref[...], w_ref[...], preferred_element_type=F32).astype(o_ref.dtype)


def _matmul(a, w, col0, n_out, out_dtype):
    m, k = a.shape
    tm, tn = _mm_tiles(m, n_out)
    j0 = col0 // tn
    out_bytes = jnp.dtype(out_dtype).itemsize
    block_bytes = 2 * tm * k * 2 + 2 * k * tn * 2 + 2 * tm * tn * out_bytes + tm * tn * 4
    return pl.pallas_call(
        _mm_plain_kernel,
        out_shape=jax.ShapeDtypeStruct((m, n_out), out_dtype),
        grid=(m // tm, n_out // tn),
        in_specs=[
            pl.BlockSpec((tm, k), lambda i, j: (i, 0)),
            pl.BlockSpec((k, tn), lambda i, j: (0, j0 + j)),
        ],
        out_specs=pl.BlockSpec((tm, tn), lambda i, j: (i, j)),
        compiler_params=pltpu.CompilerParams(
            dimension_semantics=("parallel", "parallel"), vmem_limit_bytes=_vmem_limit(block_bytes)),
        name="proj",
    )(a, w)


def _mm_headnorm_kernel(a_ref, w_ref, nw_ref, o_ref, *, n_norm_tiles, j0):
    acc = jnp.dot(a_ref[...], w_ref[...], preferred_element_type=F32)
    j = pl.program_id(1) + j0

    @pl.when(j < n_norm_tiles)
    def _():
        tm, tn = acc.shape
        for hh in range(tn // HEAD_DIM):
            blk = acc[:, hh * HEAD_DIM:(hh + 1) * HEAD_DIM]
            ms = jnp.mean(blk * blk, axis=-1, keepdims=True)
            nw = nw_ref[:, hh * HEAD_DIM:(hh + 1) * HEAD_DIM]
            o_ref[:, hh * HEAD_DIM:(hh + 1) * HEAD_DIM] = (blk * lax.rsqrt(ms + EPS) * nw).astype(o_ref.dtype)

    @pl.when(j >= n_norm_tiles)
    def _():
        o_ref[...] = acc.astype(o_ref.dtype)


def _matmul_qkv(a, w, norm_w, col0, n_out, d):
    m, k = a.shape
    tm, tn = _mm_tiles(m, min(n_out, d))
    j0 = col0 // tn
    block_bytes = 2 * tm * k * 2 + 2 * k * tn * 2 + 2 * tm * tn * 2 + 2 * tm * tn * 4
    kern = functools.partial(_mm_headnorm_kernel, n_norm_tiles=2 * d // tn, j0=j0)
    return pl.pallas_call(
        kern,
        out_shape=jax.ShapeDtypeStruct((m, n_out), BF16),
        grid=(m // tm, n_out // tn),
        in_specs=[
            pl.BlockSpec((tm, k), lambda i, j: (i, 0)),
            pl.BlockSpec((k, tn), lambda i, j: (0, j0 + j)),
            pl.BlockSpec((1, tn), lambda i, j: (0, j0 + j)),
        ],
        out_specs=pl.BlockSpec((tm, tn), lambda i, j: (i, j)),
        compiler_params=pltpu.CompilerParams(
            dimension_semantics=("parallel", "parallel"), vmem_limit_bytes=_vmem_limit(block_bytes)),
        name="proj_qkv",
    )(a, w, norm_w)


def _mm_resid_kernel(a_ref, w_ref, x_ref, gate_ref, o_ref, *, tiles_per_group, group0):
    grp = group0 + pl.program_id(0) // tiles_per_group
    acc = jnp.dot(a_ref[...], w_ref[...], preferred_element_type=F32)
    o_ref[...] = x_ref[...] + gate_ref[pl.ds(grp, 1), :] * acc


def _matmul_resid(a, w, x2, gate, rows_per_group, group0):
    m, k = a.shape
    n = w.shape[1]
    tm = min(1024, rows_per_group)
    tn = min(512, n)
    block_bytes = 2 * tm * k * 2 + 2 * k * tn * 2 + 4 * tm * tn * 4 + tm * tn * 4
    kern = functools.partial(_mm_resid_kernel, tiles_per_group=rows_per_group // tm, group0=group0)
    return pl.pallas_call(
        kern,
        out_shape=jax.ShapeDtypeStruct((m, n), F32),
        grid=(m // tm, n // tn),
        in_specs=[
            pl.BlockSpec((tm, k), lambda i, j: (i, 0)),
            pl.BlockSpec((k, tn), lambda i, j: (0, j)),
            pl.BlockSpec((tm, tn), lambda i, j: (i, j)),
            pl.BlockSpec((MOD_ROWS, tn), lambda i, j: (0, j)),
        ],
        out_specs=pl.BlockSpec((tm, tn), lambda i, j: (i, j)),
        compiler_params=pltpu.CompilerParams(
            dimension_semantics=("parallel", "parallel"), vmem_limit_bytes=_vmem_limit(block_bytes)),
        name="proj_resid",
    )(a, w, x2, gate)


CONV_PAD = SUBLANES
CONV_LEFT = CONV_W // 2


def _lru_conv(src_ref, pad_ref, dst_ref, cw_ref, cb_ref, length, tc):
    width = src_ref.shape[1]
    pad_ref[pl.ds(0, CONV_PAD), :] = jnp.zeros((CONV_PAD, width), F32)
    pad_ref[pl.ds(CONV_PAD + length, CONV_PAD), :] = jnp.zeros((CONV_PAD, width), F32)
    pad_ref[pl.ds(CONV_PAD, length), :] = src_ref[...]

    def body(c, carry):
        base = pl.multiple_of(c * tc, tc)
        win = pad_ref[pl.ds(base, tc + 2 * CONV_PAD), :]
        off = CONV_PAD - CONV_LEFT
        y = cw_ref[0:1, :] * win[off:off + tc]
        for kk in range(1, CONV_W):
            y = y + cw_ref[kk:kk + 1, :] * win[off + kk:off + kk + tc]
        dst_ref[pl.ds(base, tc), :] = y + cb_ref[...]
        return carry

    lax.fori_loop(0, length // tc, body, 0)


def _lru_coeffs(u, gw_ref, gb_ref, cvec, direction, a_ref, b_ref):
    width = u.shape[1]
    c0 = direction * 2 * width
    z = jnp.dot(u.astype(BF16), gw_ref[0, :, c0:c0 + 2 * width], preferred_element_type=F32)
    z = z + gb_ref[0, :, c0:c0 + 2 * width]
    r = jax.nn.sigmoid(z[:, :width])
    i_gate = jax.nn.sigmoid(z[:, width:])
    log_a = r * cvec
    a = jnp.exp(log_a)
    a_ref[...] = a
    b_ref[...] = jnp.sqrt(1.0 - a * a) * (i_gate * u)


def _lru_scan_chunk(a_ref, b_ref, h, out_ref, out_base, tc, reverse):
    width = a_ref.shape[1]
    row = lax.broadcasted_iota(jnp.int32, (SUBLANES, width), 0)
    n_groups = tc // SUBLANES
    order = range(n_groups - 1, -1, -1) if reverse else range(n_groups)
    for gi in order:
        ag = a_ref[pl.ds(gi * SUBLANES, SUBLANES), :]
        bg = b_ref[pl.ds(gi * SUBLANES, SUBLANES), :]
        step = 1
        while step < SUBLANES:
            shift = SUBLANES - step if reverse else step
            a_sh = pltpu.roll(ag, shift, 0)
            b_sh = pltpu.roll(bg, shift, 0)
            valid = (row < SUBLANES - step) if reverse else (row >= step)
            bg = jnp.where(valid, ag * b_sh + bg, bg)
            ag = jnp.where(valid, ag * a_sh, ag)
            step *= 2
        hg = ag * h + bg
        out_ref[pl.ds(out_base + gi * SUBLANES, SUBLANES), :] = hg
        edge = hg[0:1] if reverse else hg[SUBLANES - 1:SUBLANES]
        h = jnp.broadcast_to(edge, hg.shape)
    return h


def _rglru_kernel(u_ref, g_ref, uc_ref, gc_ref, cw_ref, cb_ref, gw_ref, gb_ref, lam_ref,
                  y_ref, yc_ref,
                  pad_l, pad_c, ucv_l, ucv_c, hf_l, hf_c, a_scr, b_scr, hb_scr, *, seq, ctx_len, tc):
    width = u_ref.shape[1]
    lam = lam_ref[...]
    softplus_neg = jnp.maximum(-lam, 0.0) + jnp.log1p(jnp.exp(-jnp.abs(lam)))
    cvec = -LRU_C * softplus_neg

    _lru_conv(uc_ref, pad_c, ucv_c, cw_ref, cb_ref, ctx_len, tc)
    _lru_conv(u_ref, pad_l, ucv_l, cw_ref, cb_ref, seq, tc)

    def forward(ucv, hf, length, h0):
        def body(c, h):
            base = pl.multiple_of(c * tc, tc)
            _lru_coeffs(ucv[pl.ds(base, tc), :], gw_ref, gb_ref, cvec[0:1], 0, a_scr, b_scr)
            return _lru_scan_chunk(a_scr, b_scr, h, hf, base, tc, reverse=False)
        return lax.fori_loop(0, length // tc, body, h0)

    def backward(ucv, hf, gate_ref, out_ref, length, h0):
        n_chunks = length // tc

        def body(cc, h):
            base = pl.multiple_of((n_chunks - 1 - cc) * tc, tc)
            _lru_coeffs(ucv[pl.ds(base, tc), :], gw_ref, gb_ref, cvec[1:2], 1, a_scr, b_scr)
            h = _lru_scan_chunk(a_scr, b_scr, h, hb_scr, 0, tc, reverse=True)
            hsum = hf[pl.ds(base, tc), :] + hb_scr[...]
            out_ref[pl.ds(base, tc), :] = (hsum * _silu(gate_ref[pl.ds(base, tc), :])).astype(out_ref.dtype)
            return h
        return lax.fori_loop(0, n_chunks, body, h0)

    zeros = jnp.zeros((SUBLANES, width), F32)
    h = forward(ucv_c, hf_c, ctx_len, zeros)
    forward(ucv_l, hf_l, seq, h)
    h = backward(ucv_c, hf_c, gc_ref, yc_ref, ctx_len, zeros)
    backward(ucv_l, hf_l, g_ref, y_ref, seq, h)


def _rglru(ug_l, ug_c, conv_w, conv_b, gate_w, gate_b, lam, batch, seq, ctx_len):
    d = conv_w.shape[1]
    nb = d // LRU_BLOCK
    w = LRU_BLOCK
    tc = min(256, ctx_len, seq)
    gw = jnp.transpose(gate_w, (2, 3, 0, 1, 4)).reshape(nb, w, 4 * w).astype(BF16)
    gb = jnp.transpose(gate_b.reshape(2, 2, nb, w), (2, 0, 1, 3)).reshape(nb, 1, 4 * w)
    block_bytes = (2 * 2 * (seq + ctx_len) * w * 4 + 2 * (seq + ctx_len) * w * 2 + 2 * w * 4 * w * 2
                   + (3 * (seq + ctx_len) + 4 * CONV_PAD + 3 * tc) * w * 4 + 8 * tc * w * 4)
    kern = functools.partial(_rglru_kernel, seq=seq, ctx_len=ctx_len, tc=tc)
    return pl.pallas_call(
        kern,
        out_shape=(jax.ShapeDtypeStruct((batch * seq, d), BF16),
                   jax.ShapeDtypeStruct((batch * ctx_len, d), BF16)),
        grid=(batch, nb),
        in_specs=[
            pl.BlockSpec((seq, w), lambda b, n: (b, n)),
            pl.BlockSpec((seq, w), lambda b, n: (b, nb + n)),
            pl.BlockSpec((ctx_len, w), lambda b, n: (b, n)),
            pl.BlockSpec((ctx_len, w), lambda b, n: (b, nb + n)),
            pl.BlockSpec((CONV_W, w), lambda b, n: (0, n)),
            pl.BlockSpec((1, w), lambda b, n: (0, n)),
            pl.BlockSpec((1, w, 4 * w), lambda b, n: (n, 0, 0)),
            pl.BlockSpec((1, 1, 4 * w), lambda b, n: (n, 0, 0)),
            pl.BlockSpec((2, w), lambda b, n: (0, n)),
        ],
        out_specs=(pl.BlockSpec((seq, w), lambda b, n: (b, n)),
                   pl.BlockSpec((ctx_len, w), lambda b, n: (b, n))),
        scratch_shapes=[
            pltpu.VMEM((seq + 2 * CONV_PAD, w), F32),
            pltpu.VMEM((ctx_len + 2 * CONV_PAD, w), F32),
            pltpu.VMEM((seq, w), F32),
            pltpu.VMEM((ctx_len, w), F32),
            pltpu.VMEM((seq, w), F32),
            pltpu.VMEM((ctx_len, w), F32),
            pltpu.VMEM((tc, w), F32),
            pltpu.VMEM((tc, w), F32),
            pltpu.VMEM((tc, w), F32),
        ],
        compiler_params=pltpu.CompilerParams(
            dimension_semantics=("parallel", "parallel"), vmem_limit_bytes=_vmem_limit(block_bytes)),
        name="rglru",
    )(ug_l, ug_l, ug_c, ug_c, conv_w, conv_b.reshape(1, d), gw, gb, lam)


NA_ROWS_PER_STEP = 4
N_ROW_VARIANTS = NA_KH


def _nt_dot(a, b):
    return lax.dot_general(a, b, (((1,), (1,)), ((), ())), preferred_element_type=F32)


def _softmax_pv(scores, values):
    m = scores[0].max(axis=-1, keepdims=True)
    for s in scores[1:]:
        m = jnp.maximum(m, s.max(axis=-1, keepdims=True))
    ps = [jnp.exp(s - m) for s in scores]
    denom = ps[0].sum(axis=-1, keepdims=True)
    for p in ps[1:]:
        denom = denom + p.sum(axis=-1, keepdims=True)
    o = jnp.dot(ps[0].astype(BF16), values[0], preferred_element_type=F32)
    for p, v in zip(ps[1:], values[1:]):
        o = o + jnp.dot(p.astype(BF16), v, preferred_element_type=F32)
    return o / denom


def _natten_kernel(*refs, rows, need_ctx):
    if need_ctx:
        q_ref, k_ref, v_ref, g_ref, kc_ref, vc_ref, bias_ref, qc_ref, gc_ref, y_ref, yc_ref = refs
    else:
        q_ref, k_ref, v_ref, g_ref, kc_ref, vc_ref, bias_ref, y_ref = refs
    kc = kc_ref[...]
    vc = vc_ref[...]
    half = NA_KH // 2

    def body(it, carry):
        for rr in range(NA_ROWS_PER_STEP):
            r = it * NA_ROWS_PER_STEP + rr
            row_start = jnp.clip(r - half, 0, rows - NA_KH)
            variant = jnp.where(r < half, r, jnp.where(r > rows - half, r - (rows - NA_KH), half))
            q0 = pl.multiple_of(r * GRID_W, GRID_W)
            k0 = pl.multiple_of(row_start * GRID_W, GRID_W)
            q = q_ref[pl.ds(q0, GRID_W), :]
            kw = k_ref[pl.ds(k0, NA_KH * GRID_W), :]
            vw = v_ref[pl.ds(k0, NA_KH * GRID_W), :]
            s_lat = _nt_dot(q, kw) + bias_ref[0, variant]
            s_ctx = _nt_dot(q, kc)
            o = _softmax_pv([s_lat, s_ctx], [vw, vc])
            y_ref[pl.ds(q0, GRID_W), :] = (o * _silu(g_ref[pl.ds(q0, GRID_W), :])).astype(y_ref.dtype)
        return carry

    lax.fori_loop(0, rows // NA_ROWS_PER_STEP, body, 0)

    if need_ctx:
        oc = _softmax_pv([_nt_dot(qc_ref[...], kc)], [vc])
        yc_ref[...] = (oc * _silu(gc_ref[...])).astype(yc_ref.dtype)


def _bias_table(rpb):
    n_heads = rpb.shape[0]
    qc = jnp.arange(GRID_W)[:, None]
    kcol = jnp.arange(GRID_W)[None, :]
    col_start = jnp.clip(qc - NA_KW // 2, 0, GRID_W - NA_KW)
    mask = (kcol >= col_start) & (kcol < col_start + NA_KW)
    col_idx = jnp.clip(kcol - qc, -(NA_KW - 1), NA_KW - 1) + (NA_KW - 1)
    row_idx = (NA_KH - 1 - jnp.arange(N_ROW_VARIANTS))[:, None] + jnp.arange(NA_KH)[None, :]
    tab = rpb[:, row_idx][:, :, :, col_idx]
    tab = jnp.where(mask[None, None, None], tab, MASK_VALUE)
    tab = jnp.transpose(tab, (0, 1, 3, 2, 4))
    return tab.reshape(n_heads, N_ROW_VARIANTS, GRID_W, NA_KH * GRID_W).astype(F32)


def _natten(qkv_l, g_l, kv_c, kv_c_head0, bias, batch, seq, ctx_len, d, q_c=None, g_c=None):
    n_heads = d // HEAD_DIM
    rows = seq // GRID_W
    need_ctx = g_c is not None
    hd = HEAD_DIM
    in_specs = [
        pl.BlockSpec((seq, hd), lambda b, h: (b, h)),
        pl.BlockSpec((seq, hd), lambda b, h: (b, n_heads + h)),
        pl.BlockSpec((seq, hd), lambda b, h: (b, 2 * n_heads + h)),
        pl.BlockSpec((seq, hd), lambda b, h: (b, h)),
        pl.BlockSpec((ctx_len, hd), lambda b, h: (b, kv_c_head0 + h)),
        pl.BlockSpec((ctx_len, hd), lambda b, h: (b, kv_c_head0 + n_heads + h)),
        pl.BlockSpec((1, N_ROW_VARIANTS, GRID_W, NA_KH * GRID_W), lambda b, h: (h, 0, 0, 0)),
    ]
    args = [qkv_l, qkv_l, qkv_l, g_l, kv_c, kv_c, bias]
    out_shape = [jax.ShapeDtypeStruct((batch * seq, d), BF16)]
    out_specs = [pl.BlockSpec((seq, hd), lambda b, h: (b, h))]
    if need_ctx:
        in_specs += [pl.BlockSpec((ctx_len, hd), lambda b, h: (b, h)),
                     pl.BlockSpec((ctx_len, hd), lambda b, h: (b, h))]
        args += [kv_c, g_c]
        out_shape.append(jax.ShapeDtypeStruct((batch * ctx_len, d), BF16))
        out_specs.append(pl.BlockSpec((ctx_len, hd), lambda b, h: (b, h)))
    block_bytes = (2 * (4 * seq * hd * 2 + seq * hd * 4 + 4 * ctx_len * hd * 4)
                   + 2 * N_ROW_VARIANTS * GRID_W * NA_KH * GRID_W * 4)
    kern = functools.partial(_natten_kernel, rows=rows, need_ctx=need_ctx)
    return pl.pallas_call(
        kern,
        out_shape=tuple(out_shape),
        grid=(batch, n_heads),
        in_specs=in_specs,
        out_specs=tuple(out_specs),
        compiler_params=pltpu.CompilerParams(
            dimension_semantics=("parallel", "parallel"), vmem_limit_bytes=_vmem_limit(block_bytes)),
        name="natten",
    )(*args)


def kernel(x, c, ctx, c_ctx, ada_w, ada_b, norm_g, lru_in_w, lru_conv_w, lru_conv_b, lru_gate_w,
           lru_gate_b, lru_lambda, lru_out_w, na_in_w, na_qk_norm, na_rpb, na_out_w):
    batch, seq, d = x.shape
    ctx_len = ctx.shape[1]
    depth = ada_w.shape[0]
    n_heads = d // HEAD_DIM
    assert batch + 1 <= MOD_ROWS and seq % GRID_W == 0 and seq // GRID_W >= 2 * NA_KH

    xl = x.reshape(batch * seq, d)
    xc = ctx.reshape(batch * ctx_len, d)
    cvec = jnp.zeros((MOD_ROWS, d), F32).at[:batch].set(c).at[batch].set(c_ctx)
    mod = _adaln(cvec, ada_w, ada_b)

    for i in range(depth):
        need_ctx = i < depth - 1
        j = i // 2
        mod_l = mod[i]
        gate = mod_l[:, 2 * d:]
        hl = _norm_mod(xl, norm_g[i], mod_l, seq, 0)
        hc = _norm_mod(xc, norm_g[i], mod_l, batch * ctx_len, batch)
        if i % 2 == 0:
            assert need_ctx
            w_in = lru_in_w[j].astype(BF16)
            ug_l = _matmul(hl, w_in, 0, 2 * d, F32)
            ug_c = _matmul(hc, w_in, 0, 2 * d, F32)
            y_l, y_c = _rglru(ug_l, ug_c, lru_conv_w[j], lru_conv_b[j], lru_gate_w[j], lru_gate_b[j],
                              lru_lambda[j], batch, seq, ctx_len)
            w_out = lru_out_w[j].astype(BF16)
        else:
            w_in = na_in_w[j].astype(BF16)
            norm_w = jnp.concatenate([jnp.tile(na_qk_norm[j, 0] * ATTN_SCALE, n_heads),
                                      jnp.tile(na_qk_norm[j, 1], n_heads),
                                      jnp.ones((d,), F32)]).reshape(1, 3 * d)
            bias = _bias_table(na_rpb[j])
            qkv_l = _matmul_qkv(hl, w_in, norm_w, 0, 3 * d, d)
            g_l = _matmul(hl, w_in, 3 * d, d, F32)
            if need_ctx:
                qkv_c = _matmul_qkv(hc, w_in, norm_w, 0, 3 * d, d)
                g_c = _matmul(hc, w_in, 3 * d, d, F32)
                y_l, y_c = _natten(qkv_l, g_l, qkv_c, n_heads, bias, batch, seq, ctx_len, d, qkv_c, g_c)
            else:
                kv_c = _matmul_qkv(hc, w_in, norm_w, d, 2 * d, d)
                (y_l,) = _natten(qkv_l, g_l, kv_c, 0, bias, batch, seq, ctx_len, d)
                y_c = None
            w_out = na_out_w[j].astype(BF16)
        xl = _matmul_resid(y_l, w_out, xl, gate, seq, 0)
        if need_ctx:
            xc = _matmul_resid(y_c, w_out, xc, gate, batch * ctx_len, batch)
    return xl.reshape(batch, seq, d)
```

```python
import functools
import math

import numpy as np
import jax
import jax.numpy as jnp
from jax import lax
from jax.experimental import pallas as pl
from jax.experimental.pallas import tpu as pltpu

GRID_W = 64
NA_KH = 8
NA_KW = 16
HEAD_DIM = 128
LRU_BLOCK = 256
CONV_W = 4
LRU_C = 8.0
EPS = 1e-6
ATTN_SCALE = HEAD_DIM ** -0.5
MASK_VALUE = -1e30

V7X_VMEM_BYTES = 64 * 1024 * 1024
SUBLANES = 8
LANES = 128
MXU_DIM = 256
MOD_ROWS = 8

F32 = jnp.float32
BF16 = jnp.bfloat16


def _vmem_limit(block_bytes):
    return int(min(block_bytes + (8 << 20), V7X_VMEM_BYTES - (4 << 20)))


def _sigmoid(v):
    return 0.5 * jnp.tanh(0.5 * v) + 0.5


def _silu(v):
    half = 0.5 * v
    return half * jnp.tanh(half) + half


def _adaln_kernel(c_ref, w_ref, b_ref, o_ref):
    cond = _silu(c_ref[...]).astype(BF16)
    w = w_ref[0].astype(BF16)
    o_ref[0] = jnp.dot(cond, w, preferred_element_type=F32) + b_ref[0]


def _adaln(cvec, ada_w, ada_b):
    n_layers, d, n3 = ada_w.shape
    tn = min(512, n3)
    block_bytes = 2 * d * tn * 4 + d * tn * 2 + 4 * MOD_ROWS * (d + 2 * tn) * 4
    return pl.pallas_call(
        _adaln_kernel,
        out_shape=jax.ShapeDtypeStruct((n_layers, MOD_ROWS, n3), F32),
        grid=(n_layers, n3 // tn),
        in_specs=[
            pl.BlockSpec((MOD_ROWS, d), lambda l, j: (0, 0)),
            pl.BlockSpec((1, d, tn), lambda l, j: (l, 0, j)),
            pl.BlockSpec((1, 1, tn), lambda l, j: (l, 0, j)),
        ],
        out_specs=pl.BlockSpec((1, MOD_ROWS, tn), lambda l, j: (l, 0, j)),
        compiler_params=pltpu.CompilerParams(
            dimension_semantics=("parallel", "parallel"), vmem_limit_bytes=_vmem_limit(block_bytes)),
        name="adaln",
    )(cvec, ada_w, ada_b.reshape(n_layers, 1, n3))


def _norm_kernel(x_ref, g_ref, mod_ref, o_ref, *, tiles_per_group, group0, d):
    grp = group0 + pl.program_id(0) // tiles_per_group
    x = x_ref[...]
    ms = jnp.mean(x * x, axis=-1, keepdims=True)
    y = x * lax.rsqrt(ms + EPS) * g_ref[0]
    shift = mod_ref[0, pl.ds(grp, 1), 0:d]
    scale = mod_ref[0, pl.ds(grp, 1), d:2 * d]
    o_ref[...] = (y * (1.0 + scale) + shift).astype(BF16)


def _norm_mod(x2, norm_g, mod, layer, rows_per_group, group0):
    m, d = x2.shape
    tm = min(256, rows_per_group)
    block_bytes = 2 * tm * d * 4 + 2 * tm * d * 2 + 2 * MOD_ROWS * 3 * d * 4 + 3 * tm * d * 4
    kern = functools.partial(_norm_kernel, tiles_per_group=rows_per_group // tm, group0=group0, d=d)
    return pl.pallas_call(
        kern,
        out_shape=jax.ShapeDtypeStruct((m, d), BF16),
        grid=(m // tm,),
        in_specs=[
            pl.BlockSpec((tm, d), lambda i: (i, 0)),
            pl.BlockSpec((1, 1, d), lambda i: (layer, 0, 0)),
            pl.BlockSpec((1, MOD_ROWS, 3 * d), lambda i: (layer, 0, 0)),
        ],
        out_specs=pl.BlockSpec((tm, d), lambda i: (i, 0)),
        compiler_params=pltpu.CompilerParams(
            dimension_semantics=("parallel",), vmem_limit_bytes=_vmem_limit(block_bytes)),
        name="norm_mod",
    )(x2, norm_g.reshape(norm_g.shape[0], 1, d), mod)


def _proj_kernel(a_ref, w_ref, *rest, mode, tiles_per_group, group0, n_norm_tiles, j0):
    if mode == "resid":
        x_ref, gate_ref, o_ref, wb = rest
    elif mode == "headnorm":
        nw_ref, o_ref, wb = rest
    else:
        o_ref, wb = rest
    j = pl.program_id(0)
    i = pl.program_id(1)

    @pl.when(i == 0)
    def _():
        wb[...] = w_ref[0].astype(BF16)

    acc = jnp.dot(a_ref[...], wb[...], preferred_element_type=F32)

    if mode == "resid":
        grp = group0 + i // tiles_per_group
        o_ref[...] = x_ref[...] + gate_ref[0, pl.ds(grp, 1), :] * acc
    elif mode == "headnorm":
        @pl.when(j + j0 < n_norm_tiles)
        def _():
            for hh in range(acc.shape[1] // HEAD_DIM):
                cols = slice(hh * HEAD_DIM, (hh + 1) * HEAD_DIM)
                blk = acc[:, cols]
                ms = jnp.mean(blk * blk, axis=-1, keepdims=True)
                o_ref[:, cols] = (blk * lax.rsqrt(ms + EPS) * nw_ref[:, cols]).astype(o_ref.dtype)

        @pl.when(j + j0 >= n_norm_tiles)
        def _():
            o_ref[...] = acc.astype(o_ref.dtype)
    else:
        o_ref[...] = acc.astype(o_ref.dtype)


def _proj(a, w_stack, layer, col0, n_out, out_dtype, *, mode="plain", norm_w=None, n_norm_cols=0,
          x2=None, mod=None, mod_layer=0, gate_col0=0, rows_per_group=None, group0=0):
    m, k = a.shape
    rows_per_group = rows_per_group or m
    tm = min(1024, rows_per_group)
    tn = min(512, n_out)
    j0 = col0 // tn
    out_bytes = jnp.dtype(out_dtype).itemsize
    block_bytes = 2 * k * tn * 4 + k * tn * 2 + 2 * tm * k * 2 + 2 * tm * tn * out_bytes + tm * tn * 4
    in_specs = [
        pl.BlockSpec((tm, k), lambda j, i: (i, 0)),
        pl.BlockSpec((1, k, tn), lambda j, i: (layer, 0, j0 + j)),
    ]
    args = [a, w_stack]
    if mode == "headnorm":
        in_specs.append(pl.BlockSpec((1, tn), lambda j, i: (0, j0 + j)))
        args.append(norm_w)
    elif mode == "resid":
        g0 = gate_col0 // tn
        in_specs += [pl.BlockSpec((tm, tn), lambda j, i: (i, j)),
                     pl.BlockSpec((1, MOD_ROWS, tn), lambda j, i: (mod_layer, 0, g0 + j))]
        args += [x2, mod]
        block_bytes += 2 * tm * tn * 4
    kern = functools.partial(_proj_kernel, mode=mode, tiles_per_group=rows_per_group // tm, group0=group0,
                             n_norm_tiles=n_norm_cols // tn, j0=j0)
    return pl.pallas_call(
        kern,
        out_shape=jax.ShapeDtypeStruct((m, n_out), out_dtype),
        grid=(n_out // tn, m // tm),
        in_specs=in_specs,
        out_specs=pl.BlockSpec((tm, tn), lambda j, i: (i, j)),
        scratch_shapes=[pltpu.VMEM((k, tn), BF16)],
        compiler_params=pltpu.CompilerParams(
            dimension_semantics=("arbitrary", "arbitrary"), vmem_limit_bytes=_vmem_limit(block_bytes)),
        name="proj_" + mode,
    )(*args)


CONV_PAD = SUBLANES
CONV_LEFT = CONV_W // 2


def _lru_conv(src_ref, pad_ref, dst_ref, cw_ref, cb_ref, length, tc):
    width = src_ref.shape[1]
    pad_ref[pl.ds(0, CONV_PAD), :] = jnp.zeros((CONV_PAD, width), F32)
    pad_ref[pl.ds(CONV_PAD + length, CONV_PAD), :] = jnp.zeros((CONV_PAD, width), F32)
    pad_ref[pl.ds(CONV_PAD, length), :] = src_ref[...]

    def body(c, carry):
        base = pl.multiple_of(c * tc, tc)
        win = pad_ref[pl.ds(base, tc + 2 * CONV_PAD), :]
        off = CONV_PAD - CONV_LEFT
        y = cw_ref[0:1, :] * win[off:off + tc]
        for kk in range(1, CONV_W):
            y = y + cw_ref[kk:kk + 1, :] * win[off + kk:off + kk + tc]
        dst_ref[pl.ds(base, tc), :] = y + cb_ref[...]
        return carry

    lax.fori_loop(0, length // tc, body, 0)


def _lru_coeffs(u, gw_ref, gb_ref, cvec_log2, direction, a_ref, b_ref):
    width = u.shape[1]
    c0 = direction * 2 * width
    z = jnp.dot(u.astype(BF16), gw_ref[0, :, c0:c0 + 2 * width], preferred_element_type=F32)
    z = z + gb_ref[0, :, c0:c0 + 2 * width]
    r = _sigmoid(z[:, :width])
    i_gate = _sigmoid(z[:, width:])
    a = jnp.exp2(r * cvec_log2)
    a_ref[...] = a
    t = 1.0 - a * a
    root = jnp.where(t > 0.0, t * lax.rsqrt(t), 0.0)
    b_ref[...] = root * (i_gate * u)


def _lru_scan_chunk(a_ref, b_ref, h, out_ref, out_base, tc, reverse):
    width = a_ref.shape[1]
    row = lax.broadcasted_iota(jnp.int32, (SUBLANES, width), 0)
    n_groups = tc // SUBLANES
    order = range(n_groups - 1, -1, -1) if reverse else range(n_groups)
    for gi in order:
        ag = a_ref[pl.ds(gi * SUBLANES, SUBLANES), :]
        bg = b_ref[pl.ds(gi * SUBLANES, SUBLANES), :]
        step = 1
        while step < SUBLANES:
            shift = SUBLANES - step if reverse else step
            a_sh = pltpu.roll(ag, shift, 0)
            b_sh = pltpu.roll(bg, shift, 0)
            valid = (row < SUBLANES - step) if reverse else (row >= step)
            bg = jnp.where(valid, ag * b_sh + bg, bg)
            ag = jnp.where(valid, ag * a_sh, ag)
            step *= 2
        hg = ag * h + bg
        out_ref[pl.ds(out_base + gi * SUBLANES, SUBLANES), :] = hg
        edge = hg[0:1] if reverse else hg[SUBLANES - 1:SUBLANES]
        h = jnp.broadcast_to(edge, hg.shape)
    return h


def _rglru_kernel(u_ref, g_ref, uc_ref, gc_ref, cw_ref, cb_ref, gw_ref, gb_ref, lam_ref,
                  y_ref, yc_ref,
                  pad_l, pad_c, ucv_l, ucv_c, hf_l, hf_c, a_scr, b_scr, hb_scr, *, seq, ctx_len, tc):
    width = u_ref.shape[1]
    lam = lam_ref[0]
    softplus_neg = jnp.maximum(-lam, 0.0) + jnp.log1p(jnp.exp(-jnp.abs(lam)))
    cvec_log2 = (-LRU_C * math.log2(math.e)) * softplus_neg

    _lru_conv(uc_ref, pad_c, ucv_c, cw_ref.at[0], cb_ref.at[0], ctx_len, tc)
    _lru_conv(u_ref, pad_l, ucv_l, cw_ref.at[0], cb_ref.at[0], seq, tc)

    def forward(ucv, hf, length, h0):
        def body(c, h):
            base = pl.multiple_of(c * tc, tc)
            _lru_coeffs(ucv[pl.ds(base, tc), :], gw_ref, gb_ref, cvec_log2[0:1], 0, a_scr, b_scr)
            return _lru_scan_chunk(a_scr, b_scr, h, hf, base, tc, reverse=False)
        return lax.fori_loop(0, length // tc, body, h0)

    def backward(ucv, hf, gate_ref, out_ref, length, h0):
        n_chunks = length // tc

        def body(cc, h):
            base = pl.multiple_of((n_chunks - 1 - cc) * tc, tc)
            _lru_coeffs(ucv[pl.ds(base, tc), :], gw_ref, gb_ref, cvec_log2[1:2], 1, a_scr, b_scr)
            h = _lru_scan_chunk(a_scr, b_scr, h, hb_scr, 0, tc, reverse=True)
            hsum = hf[pl.ds(base, tc), :] + hb_scr[...]
            out_ref[pl.ds(base, tc), :] = (hsum * _silu(gate_ref[pl.ds(base, tc), :])).astype(out_ref.dtype)
            return h
        return lax.fori_loop(0, n_chunks, body, h0)

    zeros = jnp.zeros((SUBLANES, width), F32)
    h = forward(ucv_c, hf_c, ctx_len, zeros)
    forward(ucv_l, hf_l, seq, h)
    h = backward(ucv_c, hf_c, gc_ref, yc_ref, ctx_len, zeros)
    backward(ucv_l, hf_l, g_ref, y_ref, seq, h)


def _rglru(ug_l, ug_c, conv_w, conv_b, gate_w, gate_b, lam, layer, batch, seq, ctx_len):
    d = conv_w.shape[2]
    nb = d // LRU_BLOCK
    w = LRU_BLOCK
    tc = min(256, ctx_len, seq)
    gw = jnp.transpose(gate_w[layer], (2, 3, 0, 1, 4)).reshape(nb, w, 4 * w).astype(BF16)
    gb = jnp.transpose(gate_b[layer].reshape(2, 2, nb, w), (2, 0, 1, 3)).reshape(nb, 1, 4 * w)
    block_bytes = (2 * 2 * (seq + ctx_len) * w * 4 + 2 * (seq + ctx_len) * w * 2 + 2 * w * 4 * w * 2
                   + (3 * (seq + ctx_len) + 4 * CONV_PAD + 3 * tc) * w * 4 + 8 * tc * w * 4)
    kern = functools.partial(_rglru_kernel, seq=seq, ctx_len=ctx_len, tc=tc)
    return pl.pallas_call(
        kern,
        out_shape=(jax.ShapeDtypeStruct((batch * seq, d), BF16),
                   jax.ShapeDtypeStruct((batch * ctx_len, d), BF16)),
        grid=(batch, nb),
        in_specs=[
            pl.BlockSpec((seq, w), lambda b, n: (b, n)),
            pl.BlockSpec((seq, w), lambda b, n: (b, nb + n)),
            pl.BlockSpec((ctx_len, w), lambda b, n: (b, n)),
            pl.BlockSpec((ctx_len, w), lambda b, n: (b, nb + n)),
            pl.BlockSpec((1, CONV_W, w), lambda b, n: (layer, 0, n)),
            pl.BlockSpec((1, 1, w), lambda b, n: (layer, 0, n)),
            pl.BlockSpec((1, w, 4 * w), lambda b, n: (n, 0, 0)),
            pl.BlockSpec((1, 1, 4 * w), lambda b, n: (n, 0, 0)),
            pl.BlockSpec((1, 2, w), lambda b, n: (layer, 0, n)),
        ],
        out_specs=(pl.BlockSpec((seq, w), lambda b, n: (b, n)),
                   pl.BlockSpec((ctx_len, w), lambda b, n: (b, n))),
        scratch_shapes=[
            pltpu.VMEM((seq + 2 * CONV_PAD, w), F32),
            pltpu.VMEM((ctx_len + 2 * CONV_PAD, w), F32),
            pltpu.VMEM((seq, w), F32),
            pltpu.VMEM((ctx_len, w), F32),
            pltpu.VMEM((seq, w), F32),
            pltpu.VMEM((ctx_len, w), F32),
            pltpu.VMEM((tc, w), F32),
            pltpu.VMEM((tc, w), F32),
            pltpu.VMEM((tc, w), F32),
        ],
        compiler_params=pltpu.CompilerParams(
            dimension_semantics=("parallel", "parallel"), vmem_limit_bytes=_vmem_limit(block_bytes)),
        name="rglru",
    )(ug_l, ug_l, ug_c, ug_c, conv_w, conv_b, gw, gb, lam)


NA_QROWS = 4
NA_WIN_ROWS = 12
assert NA_WIN_ROWS >= NA_QROWS + NA_KH - 1 and (NA_WIN_ROWS * GRID_W) % MXU_DIM == 0
NA_BLOCKS_PER_STEP = 2
N_BLOCK_VARIANTS = 3
MASK_SLOT = 2 * NA_KH - 1


def _nt_dot(a, b):
    return lax.dot_general(a, b, (((1,), (1,)), ((), ())), preferred_element_type=F32)


def _softmax_pv(scores, values):
    m = scores[0].max(axis=-1, keepdims=True)
    for s in scores[1:]:
        m = jnp.maximum(m, s.max(axis=-1, keepdims=True))
    ps = [jnp.exp(s - m) for s in scores]
    denom = ps[0].sum(axis=-1, keepdims=True)
    for p in ps[1:]:
        denom = denom + p.sum(axis=-1, keepdims=True)
    o = jnp.dot(ps[0].astype(BF16), values[0], preferred_element_type=F32)
    for p, v in zip(ps[1:], values[1:]):
        o = o + jnp.dot(p.astype(BF16), v, preferred_element_type=F32)
    return o / denom


def _natten_kernel(*refs, rows, need_ctx):
    if need_ctx:
        q_ref, k_ref, v_ref, g_ref, kc_ref, vc_ref, bias_ref, qc_ref, gc_ref, y_ref, yc_ref = refs
    else:
        q_ref, k_ref, v_ref, g_ref, kc_ref, vc_ref, bias_ref, y_ref = refs
    kc = kc_ref[...]
    vc = vc_ref[...]
    n_blocks = rows // NA_QROWS
    nq = NA_QROWS * GRID_W
    nk = NA_WIN_ROWS * GRID_W

    def body(it, carry):
        for sub in range(NA_BLOCKS_PER_STEP):
            blk = it * NA_BLOCKS_PER_STEP + sub
            win_start = jnp.clip(blk * NA_QROWS - NA_KH // 2, 0, rows - NA_WIN_ROWS)
            variant = jnp.where(blk == 0, 0, jnp.where(blk == n_blocks - 1, 2, 1))
            q0 = pl.multiple_of(blk * nq, nq)
            k0 = pl.multiple_of(win_start * GRID_W, GRID_W)
            q = q_ref[pl.ds(q0, nq), :]
            s_lat = _nt_dot(q, k_ref[pl.ds(k0, nk), :]) + bias_ref[0, variant]
            s_ctx = _nt_dot(q, kc)
            o = _softmax_pv([s_lat, s_ctx], [v_ref[pl.ds(k0, nk), :], vc])
            y_ref[pl.ds(q0, nq), :] = (o * _silu(g_ref[pl.ds(q0, nq), :])).astype(y_ref.dtype)
        return carry

    lax.fori_loop(0, n_blocks // NA_BLOCKS_PER_STEP, body, 0)

    if need_ctx:
        oc = _softmax_pv([_nt_dot(qc_ref[...], kc)], [vc])
        yc_ref[...] = (oc * _silu(gc_ref[...])).astype(yc_ref.dtype)


def _block_selection(rows):
    sel = np.zeros((N_BLOCK_VARIANTS, NA_QROWS, NA_WIN_ROWS, MASK_SLOT + 1), np.float32)
    n_blocks = rows // NA_QROWS
    for blk in range(n_blocks):
        variant = 0 if blk == 0 else (2 if blk == n_blocks - 1 else 1)
        win_start = min(max(blk * NA_QROWS - NA_KH // 2, 0), rows - NA_WIN_ROWS)
        pattern = np.zeros_like(sel[0])
        for qi in range(NA_QROWS):
            r = blk * NA_QROWS + qi
            row_start = min(max(r - NA_KH // 2, 0), rows - NA_KH)
            for i in range(NA_WIN_ROWS):
                kr = win_start + i
                inside = row_start <= kr < row_start + NA_KH
                pattern[qi, i, kr - r + NA_KH - 1 if inside else MASK_SLOT] = 1.0
            assert win_start <= row_start and row_start + NA_KH <= win_start + NA_WIN_ROWS
        assert not sel[variant].any() or (sel[variant] == pattern).all()
        sel[variant] = pattern
    return sel


def _bias_table(rpb, rows):
    n_heads = rpb.shape[0]
    qc = np.arange(GRID_W)[:, None]
    kcol = np.arange(GRID_W)[None, :]
    col_start = np.clip(qc - NA_KW // 2, 0, GRID_W - NA_KW)
    inside = (kcol >= col_start) & (kcol < col_start + NA_KW)
    col_idx = np.clip(kcol - qc, -(NA_KW - 1), NA_KW - 1) + (NA_KW - 1)
    onehot = (col_idx[:, :, None] == np.arange(2 * NA_KW - 1)) & inside[:, :, None]
    hi = lax.Precision.HIGHEST
    rows_tab = jnp.einsum('ckm,hjm->hjck', jnp.asarray(onehot, F32), rpb, precision=hi)
    rows_tab = rows_tab + jnp.asarray(np.where(inside, 0.0, MASK_VALUE), F32)
    masked = jnp.full((n_heads, 1, GRID_W, GRID_W), MASK_VALUE, F32)
    rows_tab = jnp.concatenate([rows_tab, masked], axis=1)
    tab = jnp.einsum('vqij,hjck->hvqcik', jnp.asarray(_block_selection(rows)), rows_tab, precision=hi)
    return tab.reshape(n_heads, N_BLOCK_VARIANTS, NA_QROWS * GRID_W, NA_WIN_ROWS * GRID_W)


def _natten(qkv_l, g_l, kv_c, kv_c_head0, bias, batch, seq, ctx_len, d, g_c=None):
    n_heads = d // HEAD_DIM
    rows = seq // GRID_W
    need_ctx = g_c is not None
    hd = HEAD_DIM
    in_specs = [
        pl.BlockSpec((seq, hd), lambda b, h: (b, h)),
        pl.BlockSpec((seq, hd), lambda b, h: (b, n_heads + h)),
        pl.BlockSpec((seq, hd), lambda b, h: (b, 2 * n_heads + h)),
        pl.BlockSpec((seq, hd), lambda b, h: (b, h)),
        pl.BlockSpec((ctx_len, hd), lambda b, h: (b, kv_c_head0 + h)),
        pl.BlockSpec((ctx_len, hd), lambda b, h: (b, kv_c_head0 + n_heads + h)),
        pl.BlockSpec((1,) + bias.shape[1:], lambda b, h: (h, 0, 0, 0)),
    ]
    args = [qkv_l, qkv_l, qkv_l, g_l, kv_c, kv_c, bias]
    out_shape = [jax.ShapeDtypeStruct((batch * seq, d), BF16)]
    out_specs = [pl.BlockSpec((seq, hd), lambda b, h: (b, h))]
    if need_ctx:
        in_specs += [pl.BlockSpec((ctx_len, hd), lambda b, h: (b, h)),
                     pl.BlockSpec((ctx_len, hd), lambda b, h: (b, h))]
        args += [kv_c, g_c]
        out_shape.append(jax.ShapeDtypeStruct((batch * ctx_len, d), BF16))
        out_specs.append(pl.BlockSpec((ctx_len, hd), lambda b, h: (b, h)))
    nq, nk = NA_QROWS * GRID_W, NA_WIN_ROWS * GRID_W
    block_bytes = (2 * (4 * seq * hd * 2 + seq * hd * 4 + 4 * ctx_len * hd * 4)
                   + 2 * N_BLOCK_VARIANTS * nq * nk * 4 + 4 * nq * (nk + ctx_len) * 4)
    kern = functools.partial(_natten_kernel, rows=rows, need_ctx=need_ctx)
    return pl.pallas_call(
        kern,
        out_shape=tuple(out_shape),
        grid=(batch, n_heads),
        in_specs=in_specs,
        out_specs=tuple(out_specs),
        compiler_params=pltpu.CompilerParams(
            dimension_semantics=("parallel", "parallel"), vmem_limit_bytes=_vmem_limit(block_bytes)),
        name="natten",
    )(*args)


def kernel(x, c, ctx, c_ctx, ada_w, ada_b, norm_g, lru_in_w, lru_conv_w, lru_conv_b, lru_gate_w,
           lru_gate_b, lru_lambda, lru_out_w, na_in_w, na_qk_norm, na_rpb, na_out_w):
    batch, seq, d = x.shape
    ctx_len = ctx.shape[1]
    depth = ada_w.shape[0]
    n_heads = d // HEAD_DIM
    rows = seq // GRID_W
    assert batch + 1 <= MOD_ROWS and seq % GRID_W == 0
    assert rows % (NA_QROWS * NA_BLOCKS_PER_STEP) == 0 and rows >= NA_WIN_ROWS + NA_QROWS

    xl = x.reshape(batch * seq, d)
    xc = ctx.reshape(batch * ctx_len, d)
    cvec = jnp.zeros((MOD_ROWS, d), F32).at[:batch].set(c).at[batch].set(c_ctx)
    mod = _adaln(cvec, ada_w, ada_b)
    lat = dict(rows_per_group=seq, group0=0)
    con = dict(rows_per_group=batch * ctx_len, group0=batch)
    conv_b = lru_conv_b.reshape(lru_conv_b.shape[0], 1, d)

    for i in range(depth):
        need_ctx = i < depth - 1
        j = i // 2
        hl = _norm_mod(xl, norm_g, mod, i, **lat)
        hc = _norm_mod(xc, norm_g, mod, i, **con)
        if i % 2 == 0:
            assert need_ctx
            ug_l = _proj(hl, lru_in_w, j, 0, 2 * d, F32)
            ug_c = _proj(hc, lru_in_w, j, 0, 2 * d, F32)
            y_l, y_c = _rglru(ug_l, ug_c, lru_conv_w, conv_b, lru_gate_w, lru_gate_b, lru_lambda,
                              j, batch, seq, ctx_len)
            w_out = lru_out_w
        else:
            norm_w = jnp.concatenate([jnp.tile(na_qk_norm[j, 0] * ATTN_SCALE, n_heads),
                                      jnp.tile(na_qk_norm[j, 1], n_heads),
                                      jnp.ones((d,), F32)]).reshape(1, 3 * d)
            bias = _bias_table(na_rpb[j], rows)
            qkv = dict(mode="headnorm", norm_w=norm_w, n_norm_cols=2 * d)
            qkv_l = _proj(hl, na_in_w, j, 0, 3 * d, BF16, **qkv)
            g_l = _proj(hl, na_in_w, j, 3 * d, d, F32)
            if need_ctx:
                qkv_c = _proj(hc, na_in_w, j, 0, 3 * d, BF16, **qkv)
                g_c = _proj(hc, na_in_w, j, 3 * d, d, F32)
                y_l, y_c = _natten(qkv_l, g_l, qkv_c, n_heads, bias, batch, seq, ctx_len, d, g_c)
            else:
                kv_c = _proj(hc, na_in_w, j, d, 2 * d, BF16, **qkv)
                (y_l,) = _natten(qkv_l, g_l, kv_c, 0, bias, batch, seq, ctx_len, d)
                y_c = None
            w_out = na_out_w
        res = dict(mode="resid", mod=mod, mod_layer=i, gate_col0=2 * d)
        xl = _proj(y_l, w_out, j, 0, d, F32, x2=xl, **res, **lat)
        if need_ctx:
            xc = _proj(y_c, w_out, j, 0, d, F32, x2=xc, **res, **con)
    return xl.reshape(batch, seq, d)
```

```python
import functools
import math

import numpy as np
import jax
import jax.numpy as jnp
from jax import lax
from jax.experimental import pallas as pl
from jax.experimental.pallas import tpu as pltpu

GRID_W = 64
NA_KH = 8
NA_KW = 16
HEAD_DIM = 128
LRU_BLOCK = 256
CONV_W = 4
LRU_C = 8.0
EPS = 1e-6
ATTN_SCALE = HEAD_DIM ** -0.5
MASK_VALUE = -1e30

V7X_VMEM_BYTES = 64 * 1024 * 1024
SUBLANES = 8
LANES = 128
MXU_DIM = 256
MOD_ROWS = 8

F32 = jnp.float32
BF16 = jnp.bfloat16


def _vmem_limit(block_bytes):
    return int(min(block_bytes + (8 << 20), V7X_VMEM_BYTES - (4 << 20)))


def _sigmoid(v):
    return 0.5 * jnp.tanh(0.5 * v) + 0.5


def _silu(v):
    half = 0.5 * v
    return half * jnp.tanh(half) + half


def _adaln_kernel(c_ref, w_ref, b_ref, o_ref):
    cond = _silu(c_ref[...]).astype(BF16)
    w = w_ref[0].astype(BF16)
    o_ref[0] = jnp.dot(cond, w, preferred_element_type=F32) + b_ref[0]


def _adaln(cvec, ada_w, ada_b):
    n_layers, d, n3 = ada_w.shape
    tn = min(512, n3)
    block_bytes = 2 * d * tn * 4 + d * tn * 2 + 4 * MOD_ROWS * (d + 2 * tn) * 4
    return pl.pallas_call(
        _adaln_kernel,
        out_shape=jax.ShapeDtypeStruct((n_layers, MOD_ROWS, n3), F32),
        grid=(n_layers, n3 // tn),
        in_specs=[
            pl.BlockSpec((MOD_ROWS, d), lambda l, j: (0, 0)),
            pl.BlockSpec((1, d, tn), lambda l, j: (l, 0, j)),
            pl.BlockSpec((1, 1, tn), lambda l, j: (l, 0, j)),
        ],
        out_specs=pl.BlockSpec((1, MOD_ROWS, tn), lambda l, j: (l, 0, j)),
        compiler_params=pltpu.CompilerParams(
            dimension_semantics=("parallel", "parallel"), vmem_limit_bytes=_vmem_limit(block_bytes)),
        name="adaln",
    )(cvec, ada_w, ada_b.reshape(n_layers, 1, n3))


def _norm_kernel(x_ref, g_ref, mod_ref, o_ref, *, tiles_per_group, group0, d):
    grp = group0 + pl.program_id(0) // tiles_per_group
    x = x_ref[...]
    ms = jnp.mean(x * x, axis=-1, keepdims=True)
    y = x * lax.rsqrt(ms + EPS) * g_ref[0]
    shift = mod_ref[0, pl.ds(grp, 1), 0:d]
    scale = mod_ref[0, pl.ds(grp, 1), d:2 * d]
    o_ref[...] = (y * (1.0 + scale) + shift).astype(BF16)


def _norm_mod(x2, norm_g, mod, layer, rows_per_group, group0):
    m, d = x2.shape
    tm = min(256, rows_per_group)
    block_bytes = 2 * tm * d * 4 + 2 * tm * d * 2 + 2 * MOD_ROWS * 3 * d * 4 + 3 * tm * d * 4
    kern = functools.partial(_norm_kernel, tiles_per_group=rows_per_group // tm, group0=group0, d=d)
    return pl.pallas_call(
        kern,
        out_shape=jax.ShapeDtypeStruct((m, d), BF16),
        grid=(m // tm,),
        in_specs=[
            pl.BlockSpec((tm, d), lambda i: (i, 0)),
            pl.BlockSpec((1, 1, d), lambda i: (layer, 0, 0)),
            pl.BlockSpec((1, MOD_ROWS, 3 * d), lambda i: (layer, 0, 0)),
        ],
        out_specs=pl.BlockSpec((tm, d), lambda i: (i, 0)),
        compiler_params=pltpu.CompilerParams(
            dimension_semantics=("parallel",), vmem_limit_bytes=_vmem_limit(block_bytes)),
        name="norm_mod",
    )(x2, norm_g.reshape(norm_g.shape[0], 1, d), mod)


def _proj_epilogue(acc, o_ref, extra, row_tile, col_tile, *, mode, tiles_per_group, group0, n_norm_tiles):
    if mode == "resid":
        x_ref, gate_ref = extra
        grp = group0 + row_tile // tiles_per_group
        o_ref[...] = x_ref[...] + gate_ref[0, pl.ds(grp, 1), :] * acc
    elif mode == "headnorm":
        (nw_ref,) = extra

        @pl.when(col_tile < n_norm_tiles)
        def _():
            for hh in range(acc.shape[1] // HEAD_DIM):
                cols = slice(hh * HEAD_DIM, (hh + 1) * HEAD_DIM)
                blk = acc[:, cols]
                ms = jnp.mean(blk * blk, axis=-1, keepdims=True)
                o_ref[:, cols] = (blk * lax.rsqrt(ms + EPS) * nw_ref[:, cols]).astype(o_ref.dtype)

        @pl.when(col_tile >= n_norm_tiles)
        def _():
            o_ref[...] = acc.astype(o_ref.dtype)
    else:
        o_ref[...] = acc.astype(o_ref.dtype)


def _proj_kernel(a_ref, w_ref, *rest, j0, **epilogue):
    *extra, o_ref, wb = rest
    j = pl.program_id(0)
    i = pl.program_id(1)

    @pl.when(i == 0)
    def _():
        wb[...] = w_ref[0].astype(BF16)

    acc = jnp.dot(a_ref[...], wb[...], preferred_element_type=F32)
    _proj_epilogue(acc, o_ref, extra, i, j + j0, **epilogue)


def _proj_stream_kernel(a_ref, w_ref, *rest, j0, chunk_rows, **epilogue):
    *extra, o_ref, wb = rest
    jj = pl.program_id(0)
    i = pl.program_id(1)

    def round_chunk():
        r0 = pl.multiple_of(i * chunk_rows, chunk_rows)
        wb[jj % 2, pl.ds(r0, chunk_rows), :] = w_ref[0].astype(BF16)

    @pl.when(jj == 0)
    def _():
        round_chunk()

    @pl.when(jj > 0)
    def _():
        round_chunk()
        acc = jnp.dot(a_ref[...], wb[(jj - 1) % 2], preferred_element_type=F32)
        _proj_epilogue(acc, o_ref, extra, i, jj - 1 + j0, **epilogue)


def _proj(a, w_stack, layer, col0, n_out, out_dtype, *, mode="plain", norm_w=None, n_norm_cols=0,
          x2=None, mod=None, mod_layer=0, gate_col0=0, rows_per_group=None, group0=0):
    m, k = a.shape
    rows_per_group = rows_per_group or m
    tm = min(1024, rows_per_group)
    n_row_tiles = m // tm
    stream = n_row_tiles > 1
    tn = 512 if mode == "resid" or not stream else 1024
    while any(extent % tn for extent in (n_out, col0, n_norm_cols, gate_col0)):
        tn //= 2
    assert tn >= LANES
    n_col_tiles = n_out // tn
    j0 = col0 // tn
    out_bytes = jnp.dtype(out_dtype).itemsize
    block_bytes = 2 * tm * k * 2 + 2 * tm * tn * out_bytes + tm * tn * 4
    epilogue = dict(mode=mode, tiles_per_group=rows_per_group // tm, group0=group0,
                    n_norm_tiles=n_norm_cols // tn)
    if stream:
        assert k % n_row_tiles == 0
        chunk_rows = k // n_row_tiles
        block_bytes += 2 * chunk_rows * tn * 4 + 2 * k * tn * 2
        grid = (n_col_tiles + 1, n_row_tiles)

        def row_of(jj, i):
            return jnp.where(jj == 0, 0, i)

        def col_of(jj):
            return jnp.maximum(jj - 1, 0)

        in_specs = [
            pl.BlockSpec((tm, k), lambda jj, i: (row_of(jj, i), 0)),
            pl.BlockSpec((1, chunk_rows, tn), lambda jj, i: (
                layer, jnp.where(jj == n_col_tiles, 0, i), j0 + jnp.minimum(jj, n_col_tiles - 1))),
        ]
        scratch = pltpu.VMEM((2, k, tn), BF16)
        kern = functools.partial(_proj_stream_kernel, j0=j0, chunk_rows=chunk_rows, **epilogue)
    else:
        block_bytes += 2 * k * tn * 4 + k * tn * 2
        grid = (n_col_tiles, n_row_tiles)

        def row_of(jj, i):
            return i

        def col_of(jj):
            return jj

        in_specs = [
            pl.BlockSpec((tm, k), lambda jj, i: (i, 0)),
            pl.BlockSpec((1, k, tn), lambda jj, i: (layer, 0, j0 + jj)),
        ]
        scratch = pltpu.VMEM((k, tn), BF16)
        kern = functools.partial(_proj_kernel, j0=j0, **epilogue)
    args = [a, w_stack]
    if mode == "headnorm":
        in_specs.append(pl.BlockSpec((1, tn), lambda jj, i: (0, j0 + col_of(jj))))
        args.append(norm_w)
    elif mode == "resid":
        g0 = gate_col0 // tn
        in_specs += [pl.BlockSpec((tm, tn), lambda jj, i: (row_of(jj, i), col_of(jj))),
                     pl.BlockSpec((1, MOD_ROWS, tn), lambda jj, i: (mod_layer, 0, g0 + col_of(jj)))]
        args += [x2, mod]
        block_bytes += 2 * tm * tn * 4
    return pl.pallas_call(
        kern,
        out_shape=jax.ShapeDtypeStruct((m, n_out), out_dtype),
        grid=grid,
        in_specs=in_specs,
        out_specs=pl.BlockSpec((tm, tn), lambda jj, i: (row_of(jj, i), col_of(jj))),
        scratch_shapes=[scratch],
        compiler_params=pltpu.CompilerParams(
            dimension_semantics=("arbitrary", "arbitrary"), vmem_limit_bytes=_vmem_limit(block_bytes)),
        name="proj_" + mode,
    )(*args)


CONV_PAD = SUBLANES
CONV_LEFT = CONV_W // 2


def _lru_conv(src_ref, pad_ref, dst_ref, cw_ref, cb_ref, length, tc):
    width = src_ref.shape[1]
    pad_ref[pl.ds(0, CONV_PAD), :] = jnp.zeros((CONV_PAD, width), F32)
    pad_ref[pl.ds(CONV_PAD + length, CONV_PAD), :] = jnp.zeros((CONV_PAD, width), F32)
    pad_ref[pl.ds(CONV_PAD, length), :] = src_ref[...]

    def body(c, carry):
        base = pl.multiple_of(c * tc, tc)
        win = pad_ref[pl.ds(base, tc + 2 * CONV_PAD), :]
        off = CONV_PAD - CONV_LEFT
        y = cw_ref[0:1, :] * win[off:off + tc]
        for kk in range(1, CONV_W):
            y = y + cw_ref[kk:kk + 1, :] * win[off + kk:off + kk + tc]
        dst_ref[pl.ds(base, tc), :] = y + cb_ref[...]
        return carry

    lax.fori_loop(0, length // tc, body, 0)


def _lru_coeffs(u, gw_ref, gb_ref, cvec_log2, direction, a_ref, b_ref):
    width = u.shape[1]
    c0 = direction * 2 * width
    z = jnp.dot(u.astype(BF16), gw_ref[0, :, c0:c0 + 2 * width], preferred_element_type=F32)
    z = z + gb_ref[0, :, c0:c0 + 2 * width]
    r = _sigmoid(z[:, :width])
    i_gate = _sigmoid(z[:, width:])
    a = jnp.exp2(r * cvec_log2)
    a_ref[...] = a
    t = 1.0 - a * a
    root = jnp.where(t > 0.0, t * lax.rsqrt(t), 0.0)
    b_ref[...] = root * (i_gate * u)


def _lru_scan_chunk(a_ref, b_ref, h, out_ref, out_base, tc, reverse):
    width = a_ref.shape[1]
    row = lax.broadcasted_iota(jnp.int32, (SUBLANES, width), 0)
    n_groups = tc // SUBLANES
    order = range(n_groups - 1, -1, -1) if reverse else range(n_groups)
    for gi in order:
        ag = a_ref[pl.ds(gi * SUBLANES, SUBLANES), :]
        bg = b_ref[pl.ds(gi * SUBLANES, SUBLANES), :]
        step = 1
        while step < SUBLANES:
            shift = SUBLANES - step if reverse else step
            a_sh = pltpu.roll(ag, shift, 0)
            b_sh = pltpu.roll(bg, shift, 0)
            valid = (row < SUBLANES - step) if reverse else (row >= step)
            bg = jnp.where(valid, ag * b_sh + bg, bg)
            ag = jnp.where(valid, ag * a_sh, ag)
            step *= 2
        hg = ag * h + bg
        out_ref[pl.ds(out_base + gi * SUBLANES, SUBLANES), :] = hg
        edge = hg[0:1] if reverse else hg[SUBLANES - 1:SUBLANES]
        h = jnp.broadcast_to(edge, hg.shape)
    return h


def _rglru_kernel(u_ref, g_ref, uc_ref, gc_ref, cw_ref, cb_ref, gw_ref, gb_ref, lam_ref,
                  y_ref, yc_ref,
                  pad_l, pad_c, ucv_l, ucv_c, hf_l, hf_c, a_scr, b_scr, hb_scr, *, seq, ctx_len, tc):
    width = u_ref.shape[1]
    lam = lam_ref[0]
    softplus_neg = jnp.maximum(-lam, 0.0) + jnp.log1p(jnp.exp(-jnp.abs(lam)))
    cvec_log2 = (-LRU_C * math.log2(math.e)) * softplus_neg

    _lru_conv(uc_ref, pad_c, ucv_c, cw_ref.at[0], cb_ref.at[0], ctx_len, tc)
    _lru_conv(u_ref, pad_l, ucv_l, cw_ref.at[0], cb_ref.at[0], seq, tc)

    def forward(ucv, hf, length, h0):
        def body(c, h):
            base = pl.multiple_of(c * tc, tc)
            _lru_coeffs(ucv[pl.ds(base, tc), :], gw_ref, gb_ref, cvec_log2[0:1], 0, a_scr, b_scr)
            return _lru_scan_chunk(a_scr, b_scr, h, hf, base, tc, reverse=False)
        return lax.fori_loop(0, length // tc, body, h0)

    def backward(ucv, hf, gate_ref, out_ref, length, h0):
        n_chunks = length // tc

        def body(cc, h):
            base = pl.multiple_of((n_chunks - 1 - cc) * tc, tc)
            _lru_coeffs(ucv[pl.ds(base, tc), :], gw_ref, gb_ref, cvec_log2[1:2], 1, a_scr, b_scr)
            h = _lru_scan_chunk(a_scr, b_scr, h, hb_scr, 0, tc, reverse=True)
            hsum = hf[pl.ds(base, tc), :] + hb_scr[...]
            out_ref[pl.ds(base, tc), :] = (hsum * _silu(gate_ref[pl.ds(base, tc), :])).astype(out_ref.dtype)
            return h
        return lax.fori_loop(0, n_chunks, body, h0)

    zeros = jnp.zeros((SUBLANES, width), F32)
    h = forward(ucv_c, hf_c, ctx_len, zeros)
    forward(ucv_l, hf_l, seq, h)
    h = backward(ucv_c, hf_c, gc_ref, yc_ref, ctx_len, zeros)
    backward(ucv_l, hf_l, g_ref, y_ref, seq, h)


def _rglru(ug_l, ug_c, conv_w, conv_b, gate_w, gate_b, lam, layer, batch, seq, ctx_len):
    d = conv_w.shape[2]
    nb = d // LRU_BLOCK
    w = LRU_BLOCK
    tc = min(256, ctx_len, seq)
    gw = jnp.transpose(gate_w[layer], (2, 3, 0, 1, 4)).reshape(nb, w, 4 * w).astype(BF16)
    gb = jnp.transpose(gate_b[layer].reshape(2, 2, nb, w), (2, 0, 1, 3)).reshape(nb, 1, 4 * w)
    block_bytes = (2 * 2 * (seq + ctx_len) * w * 4 + 2 * (seq + ctx_len) * w * 2 + 2 * w * 4 * w * 2
                   + (3 * (seq + ctx_len) + 4 * CONV_PAD + 3 * tc) * w * 4 + 8 * tc * w * 4)
    kern = functools.partial(_rglru_kernel, seq=seq, ctx_len=ctx_len, tc=tc)
    return pl.pallas_call(
        kern,
        out_shape=(jax.ShapeDtypeStruct((batch * seq, d), BF16),
                   jax.ShapeDtypeStruct((batch * ctx_len, d), BF16)),
        grid=(batch, nb),
        in_specs=[
            pl.BlockSpec((seq, w), lambda b, n: (b, n)),
            pl.BlockSpec((seq, w), lambda b, n: (b, nb + n)),
            pl.BlockSpec((ctx_len, w), lambda b, n: (b, n)),
            pl.BlockSpec((ctx_len, w), lambda b, n: (b, nb + n)),
            pl.BlockSpec((1, CONV_W, w), lambda b, n: (layer, 0, n)),
            pl.BlockSpec((1, 1, w), lambda b, n: (layer, 0, n)),
            pl.BlockSpec((1, w, 4 * w), lambda b, n: (n, 0, 0)),
            pl.BlockSpec((1, 1, 4 * w), lambda b, n: (n, 0, 0)),
            pl.BlockSpec((1, 2, w), lambda b, n: (layer, 0, n)),
        ],
        out_specs=(pl.BlockSpec((seq, w), lambda b, n: (b, n)),
                   pl.BlockSpec((ctx_len, w), lambda b, n: (b, n))),
        scratch_shapes=[
            pltpu.VMEM((seq + 2 * CONV_PAD, w), F32),
            pltpu.VMEM((ctx_len + 2 * CONV_PAD, w), F32),
            pltpu.VMEM((seq, w), F32),
            pltpu.VMEM((ctx_len, w), F32),
            pltpu.VMEM((seq, w), F32),
            pltpu.VMEM((ctx_len, w), F32),
            pltpu.VMEM((tc, w), F32),
            pltpu.VMEM((tc, w), F32),
            pltpu.VMEM((tc, w), F32),
        ],
        compiler_params=pltpu.CompilerParams(
            dimension_semantics=("parallel", "parallel"), vmem_limit_bytes=_vmem_limit(block_bytes)),
        name="rglru",
    )(ug_l, ug_l, ug_c, ug_c, conv_w, conv_b, gw, gb, lam)


NA_QROWS = 4
NA_WIN_ROWS = 12
assert NA_WIN_ROWS >= NA_QROWS + NA_KH - 1 and (NA_WIN_ROWS * GRID_W) % MXU_DIM == 0
NA_BLOCKS_PER_STEP = 2
NA_TILE_ROWS = LANES // GRID_W
assert NA_TILE_ROWS == 2 and NA_WIN_ROWS % NA_TILE_ROWS == 0
N_BIAS_ROWS = 2 * NA_KH - 1
BIAS_BLOCKS = N_BIAS_ROWS + 3


def _nt_dot(a, b):
    return lax.dot_general(a, b, (((1,), (1,)), ((), ())), preferred_element_type=F32)


def _softmax_pv(scores, values):
    m = scores[0].max(axis=-1, keepdims=True)
    for s in scores[1:]:
        m = jnp.maximum(m, s.max(axis=-1, keepdims=True))
    ps = [jnp.exp(s - m) for s in scores]
    denom = ps[0].sum(axis=-1, keepdims=True)
    for p in ps[1:]:
        denom = denom + p.sum(axis=-1, keepdims=True)
    o = jnp.dot(ps[0].astype(BF16), values[0], preferred_element_type=F32)
    for p, v in zip(ps[1:], values[1:]):
        o = o + jnp.dot(p.astype(BF16), v, preferred_element_type=F32)
    return o / denom


def _tile_plan(rows):
    n_blocks = rows // NA_QROWS
    plans = {}
    for blk in range(n_blocks):
        kind = "first" if blk == 0 else ("last" if blk == n_blocks - 1 else "inner")
        win_start = min(max(blk * NA_QROWS - NA_KH // 2, 0), rows - NA_WIN_ROWS)
        plan = []
        for qi in range(NA_QROWS):
            r = blk * NA_QROWS + qi
            row_start = min(max(r - NA_KH // 2, 0), rows - NA_KH)
            assert win_start <= row_start and row_start + NA_KH <= win_start + NA_WIN_ROWS
            tiles = []
            for t in range(NA_WIN_ROWS // NA_TILE_ROWS):
                key_rows = [win_start + t * NA_TILE_ROWS + e for e in range(NA_TILE_ROWS)]
                inside = [row_start <= kr < row_start + NA_KH for kr in key_rows]
                if not any(inside):
                    tiles.append(None)
                    continue
                e0 = inside.index(True)
                block = 1 + (key_rows[e0] - r + NA_KH - 1) - e0
                assert 0 <= block and block + NA_TILE_ROWS <= BIAS_BLOCKS - 1
                keep = "both" if all(inside) else ("left" if inside[0] else "right")
                tiles.append((block % 2, (block - block % 2) * GRID_W, keep))
            plan.append(tuple(tiles))
        assert plans.setdefault(kind, tuple(plan)) == tuple(plan)
    return plans


def _attend_block(q, kw, vw, kc, vc, bias_ref, plan):
    s_lat = _nt_dot(q, kw)
    s_ctx = _nt_dot(q, kc)
    lane = lax.broadcasted_iota(jnp.int32, (GRID_W, LANES), 1)
    p_lat_rows, p_ctx_rows, denoms = [], [], []
    for qi, tiles in enumerate(plan):
        rs = slice(qi * GRID_W, (qi + 1) * GRID_W)
        scored = []
        for t, spec in enumerate(tiles):
            if spec is None:
                scored.append(None)
                continue
            copy, off, keep = spec
            s = s_lat[rs, t * LANES:(t + 1) * LANES] + bias_ref[0, copy, :, off:off + LANES]
            if keep == "left":
                s = jnp.where(lane < GRID_W, s, MASK_VALUE)
            elif keep == "right":
                s = jnp.where(lane >= GRID_W, s, MASK_VALUE)
            scored.append(s)
        ctx_tiles = [s_ctx[rs, t * LANES:(t + 1) * LANES] for t in range(s_ctx.shape[1] // LANES)]
        live = [s for s in scored if s is not None] + ctx_tiles
        m = functools.reduce(jnp.maximum, live).max(axis=-1, keepdims=True)
        p_lat = [None if s is None else jnp.exp(s - m) for s in scored]
        p_ctx = [jnp.exp(s - m) for s in ctx_tiles]
        total = functools.reduce(jnp.add, [p for p in p_lat if p is not None] + p_ctx)
        denoms.append(total.sum(axis=-1, keepdims=True))
        zero = jnp.zeros((GRID_W, LANES), BF16)
        p_lat_rows.append(jnp.concatenate([zero if p is None else p.astype(BF16) for p in p_lat], axis=1))
        p_ctx_rows.append(jnp.concatenate([p.astype(BF16) for p in p_ctx], axis=1))
    o = jnp.dot(jnp.concatenate(p_lat_rows, axis=0), vw, preferred_element_type=F32)
    o = o + jnp.dot(jnp.concatenate(p_ctx_rows, axis=0), vc, preferred_element_type=F32)
    return o / jnp.concatenate(denoms, axis=0)


def _natten_kernel(*refs, rows, need_ctx):
    if need_ctx:
        q_ref, k_ref, v_ref, g_ref, kc_ref, vc_ref, bias_ref, qc_ref, gc_ref, y_ref, yc_ref = refs
    else:
        q_ref, k_ref, v_ref, g_ref, kc_ref, vc_ref, bias_ref, y_ref = refs
    kc = kc_ref[...]
    vc = vc_ref[...]
    n_blocks = rows // NA_QROWS
    nq = NA_QROWS * GRID_W
    nk = NA_WIN_ROWS * GRID_W
    plans = _tile_plan(rows)

    def attend(blk, kind):
        if isinstance(blk, int):
            q0 = blk * nq
            k0 = min(max(blk * NA_QROWS - NA_KH // 2, 0), rows - NA_WIN_ROWS) * GRID_W
        else:
            q0 = pl.multiple_of(blk * nq, nq)
            k0 = pl.multiple_of(jnp.clip(blk * NA_QROWS - NA_KH // 2, 0, rows - NA_WIN_ROWS) * GRID_W, GRID_W)
        o = _attend_block(q_ref[pl.ds(q0, nq), :], k_ref[pl.ds(k0, nk), :], v_ref[pl.ds(k0, nk), :],
                          kc, vc, bias_ref, plans[kind])
        y_ref[pl.ds(q0, nq), :] = (o * _silu(g_ref[pl.ds(q0, nq), :])).astype(y_ref.dtype)

    def inner(it, carry):
        for sub in range(NA_BLOCKS_PER_STEP):
            attend(1 + it * NA_BLOCKS_PER_STEP + sub, "inner")
        return carry

    attend(0, "first")
    lax.fori_loop(0, (n_blocks - 2) // NA_BLOCKS_PER_STEP, inner, 0)
    attend(n_blocks - 1, "last")

    if need_ctx:
        oc = _softmax_pv([_nt_dot(qc_ref[...], kc)], [vc])
        yc_ref[...] = (oc * _silu(gc_ref[...])).astype(yc_ref.dtype)


def _bias_table(rpb):
    n_heads = rpb.shape[0]
    qc = np.arange(GRID_W)[:, None]
    kcol = np.arange(GRID_W)[None, :]
    col_start = np.clip(qc - NA_KW // 2, 0, GRID_W - NA_KW)
    inside = (kcol >= col_start) & (kcol < col_start + NA_KW)
    col_idx = np.clip(kcol - qc, -(NA_KW - 1), NA_KW - 1) + (NA_KW - 1)
    onehot = (col_idx[:, :, None] == np.arange(2 * NA_KW - 1)) & inside[:, :, None]
    tab = jnp.einsum('ckm,hjm->hcjk', jnp.asarray(onehot, F32), rpb, precision=lax.Precision.HIGHEST)
    tab = tab + jnp.asarray(np.where(inside, 0.0, MASK_VALUE), F32)[None, :, None, :]
    tab = tab.reshape(n_heads, GRID_W, N_BIAS_ROWS * GRID_W)

    def padded(left_blocks):
        right_blocks = BIAS_BLOCKS - N_BIAS_ROWS - left_blocks
        return jnp.pad(tab, ((0, 0), (0, 0), (left_blocks * GRID_W, right_blocks * GRID_W)),
                       constant_values=MASK_VALUE)

    return jnp.stack([padded(1), padded(0)], axis=1)


def _natten(qkv_l, g_l, kv_c, kv_c_head0, bias, batch, seq, ctx_len, d, g_c=None):
    n_heads = d // HEAD_DIM
    rows = seq // GRID_W
    need_ctx = g_c is not None
    hd = HEAD_DIM
    in_specs = [
        pl.BlockSpec((seq, hd), lambda b, h: (b, h)),
        pl.BlockSpec((seq, hd), lambda b, h: (b, n_heads + h)),
        pl.BlockSpec((seq, hd), lambda b, h: (b, 2 * n_heads + h)),
        pl.BlockSpec((seq, hd), lambda b, h: (b, h)),
        pl.BlockSpec((ctx_len, hd), lambda b, h: (b, kv_c_head0 + h)),
        pl.BlockSpec((ctx_len, hd), lambda b, h: (b, kv_c_head0 + n_heads + h)),
        pl.BlockSpec((1,) + bias.shape[1:], lambda b, h: (h, 0, 0, 0)),
    ]
    args = [qkv_l, qkv_l, qkv_l, g_l, kv_c, kv_c, bias]
    out_shape = [jax.ShapeDtypeStruct((batch * seq, d), BF16)]
    out_specs = [pl.BlockSpec((seq, hd), lambda b, h: (b, h))]
    if need_ctx:
        in_specs += [pl.BlockSpec((ctx_len, hd), lambda b, h: (b, h)),
                     pl.BlockSpec((ctx_len, hd), lambda b, h: (b, h))]
        args += [kv_c, g_c]
        out_shape.append(jax.ShapeDtypeStruct((batch * ctx_len, d), BF16))
        out_specs.append(pl.BlockSpec((ctx_len, hd), lambda b, h: (b, h)))
    nq, nk = NA_QROWS * GRID_W, NA_WIN_ROWS * GRID_W
    block_bytes = (2 * (4 * seq * hd * 2 + seq * hd * 4 + 4 * ctx_len * hd * 4)
                   + 2 * bias[0].size * 4 + 4 * NA_BLOCKS_PER_STEP * nq * (nk + ctx_len) * 4)
    kern = functools.partial(_natten_kernel, rows=rows, need_ctx=need_ctx)
    return pl.pallas_call(
        kern,
        out_shape=tuple(out_shape),
        grid=(batch, n_heads),
        in_specs=in_specs,
        out_specs=tuple(out_specs),
        compiler_params=pltpu.CompilerParams(
            dimension_semantics=("parallel", "parallel"), vmem_limit_bytes=_vmem_limit(block_bytes)),
        name="natten",
    )(*args)


def kernel(x, c, ctx, c_ctx, ada_w, ada_b, norm_g, lru_in_w, lru_conv_w, lru_conv_b, lru_gate_w,
           lru_gate_b, lru_lambda, lru_out_w, na_in_w, na_qk_norm, na_rpb, na_out_w):
    batch, seq, d = x.shape
    ctx_len = ctx.shape[1]
    depth = ada_w.shape[0]
    n_heads = d // HEAD_DIM
    rows = seq // GRID_W
    assert batch + 1 <= MOD_ROWS and seq % GRID_W == 0
    assert rows % NA_QROWS == 0 and rows >= NA_WIN_ROWS + NA_QROWS
    assert (rows // NA_QROWS - 2) % NA_BLOCKS_PER_STEP == 0 and ctx_len % LANES == 0

    xl = x.reshape(batch * seq, d)
    xc = ctx.reshape(batch * ctx_len, d)
    cvec = jnp.zeros((MOD_ROWS, d), F32).at[:batch].set(c).at[batch].set(c_ctx)
    mod = _adaln(cvec, ada_w, ada_b)
    lat = dict(rows_per_group=seq, group0=0)
    con = dict(rows_per_group=batch * ctx_len, group0=batch)
    conv_b = lru_conv_b.reshape(lru_conv_b.shape[0], 1, d)

    for i in range(depth):
        need_ctx = i < depth - 1
        j = i // 2
        hl = _norm_mod(xl, norm_g, mod, i, **lat)
        hc = _norm_mod(xc, norm_g, mod, i, **con)
        if i % 2 == 0:
            assert need_ctx
            ug_l = _proj(hl, lru_in_w, j, 0, 2 * d, F32)
            ug_c = _proj(hc, lru_in_w, j, 0, 2 * d, F32)
            y_l, y_c = _rglru(ug_l, ug_c, lru_conv_w, conv_b, lru_gate_w, lru_gate_b, lru_lambda,
                              j, batch, seq, ctx_len)
            w_out = lru_out_w
        else:
            norm_w = jnp.concatenate([jnp.tile(na_qk_norm[j, 0] * ATTN_SCALE, n_heads),
                                      jnp.tile(na_qk_norm[j, 1], n_heads),
                                      jnp.ones((d,), F32)]).reshape(1, 3 * d)
            bias = _bias_table(na_rpb[j])
            qkv = dict(mode="headnorm", norm_w=norm_w, n_norm_cols=2 * d)
            qkv_l = _proj(hl, na_in_w, j, 0, 3 * d, BF16, **qkv)
            g_l = _proj(hl, na_in_w, j, 3 * d, d, F32)
            if need_ctx:
                qkv_c = _proj(hc, na_in_w, j, 0, 3 * d, BF16, **qkv)
                g_c = _proj(hc, na_in_w, j, 3 * d, d, F32)
                y_l, y_c = _natten(qkv_l, g_l, qkv_c, n_heads, bias, batch, seq, ctx_len, d, g_c)
            else:
                kv_c = _proj(hc, na_in_w, j, d, 2 * d, BF16, **qkv)
                (y_l,) = _natten(qkv_l, g_l, kv_c, 0, bias, batch, seq, ctx_len, d)
                y_c = None
            w_out = na_out_w
        res = dict(mode="resid", mod=mod, mod_layer=i, gate_col0=2 * d)
        xl = _proj(y_l, w_out, j, 0, d, F32, x2=xl, **res, **lat)
        if need_ctx:
            xc = _proj(y_c, w_out, j, 0, d, F32, x2=xc, **res, **con)
    return xl.reshape(batch, seq, d)
```

```python
import functools
import math

import numpy as np
import jax
import jax.numpy as jnp
from jax import lax
from jax.experimental import pallas as pl
from jax.experimental.pallas import tpu as pltpu

GRID_W = 64
NA_KH = 8
NA_KW = 16
HEAD_DIM = 128
LRU_BLOCK = 256
CONV_W = 4
LRU_C = 8.0
EPS = 1e-6
ATTN_SCALE = HEAD_DIM ** -0.5
MASK_VALUE = -1e30

V7X_VMEM_BYTES = 64 * 1024 * 1024
SUBLANES = 8
LANES = 128
MXU_DIM = 256
MOD_ROWS = 8

F32 = jnp.float32
BF16 = jnp.bfloat16


def _vmem_limit(block_bytes):
    return int(min(block_bytes + (8 << 20), V7X_VMEM_BYTES - (4 << 20)))


def _sigmoid(v):
    return 0.5 * jnp.tanh(0.5 * v) + 0.5


def _silu(v):
    half = 0.5 * v
    return half * jnp.tanh(half) + half


def _adaln_kernel(c_ref, w_ref, b_ref, o_ref):
    cond = _silu(c_ref[...]).astype(BF16)
    w = w_ref[0].astype(BF16)
    o_ref[0] = jnp.dot(cond, w, preferred_element_type=F32) + b_ref[0]


def _adaln(cvec, ada_w, ada_b):
    n_layers, d, n3 = ada_w.shape
    tn = min(512, n3)
    block_bytes = 2 * d * tn * 4 + d * tn * 2 + 4 * MOD_ROWS * (d + 2 * tn) * 4
    return pl.pallas_call(
        _adaln_kernel,
        out_shape=jax.ShapeDtypeStruct((n_layers, MOD_ROWS, n3), F32),
        grid=(n_layers, n3 // tn),
        in_specs=[
            pl.BlockSpec((MOD_ROWS, d), lambda l, j: (0, 0)),
            pl.BlockSpec((1, d, tn), lambda l, j: (l, 0, j)),
            pl.BlockSpec((1, 1, tn), lambda l, j: (l, 0, j)),
        ],
        out_specs=pl.BlockSpec((1, MOD_ROWS, tn), lambda l, j: (l, 0, j)),
        compiler_params=pltpu.CompilerParams(
            dimension_semantics=("parallel", "parallel"), vmem_limit_bytes=_vmem_limit(block_bytes)),
        name="adaln",
    )(cvec, ada_w, ada_b.reshape(n_layers, 1, n3))


def _norm_kernel(x_ref, g_ref, mod_ref, o_ref, *, tiles_per_group, group0, d):
    grp = group0 + pl.program_id(0) // tiles_per_group
    x = x_ref[...]
    ms = jnp.mean(x * x, axis=-1, keepdims=True)
    y = x * lax.rsqrt(ms + EPS) * g_ref[0]
    shift = mod_ref[0, pl.ds(grp, 1), 0:d]
    scale = mod_ref[0, pl.ds(grp, 1), d:2 * d]
    o_ref[...] = (y * (1.0 + scale) + shift).astype(BF16)


def _norm_mod(x2, norm_g, mod, layer, rows_per_group, group0):
    m, d = x2.shape
    tm = min(256, rows_per_group)
    block_bytes = 2 * tm * d * 4 + 2 * tm * d * 2 + 2 * MOD_ROWS * 3 * d * 4 + 3 * tm * d * 4
    kern = functools.partial(_norm_kernel, tiles_per_group=rows_per_group // tm, group0=group0, d=d)
    return pl.pallas_call(
        kern,
        out_shape=jax.ShapeDtypeStruct((m, d), BF16),
        grid=(m // tm,),
        in_specs=[
            pl.BlockSpec((tm, d), lambda i: (i, 0)),
            pl.BlockSpec((1, 1, d), lambda i: (layer, 0, 0)),
            pl.BlockSpec((1, MOD_ROWS, 3 * d), lambda i: (layer, 0, 0)),
        ],
        out_specs=pl.BlockSpec((tm, d), lambda i: (i, 0)),
        compiler_params=pltpu.CompilerParams(
            dimension_semantics=("parallel",), vmem_limit_bytes=_vmem_limit(block_bytes)),
        name="norm_mod",
    )(x2, norm_g.reshape(norm_g.shape[0], 1, d), mod)


def _proj_epilogue(acc, o_ref, extra, row_tile, col_tile, *, mode, tiles_per_group, group0, n_norm_tiles):
    if mode == "resid":
        x_ref, gate_ref = extra
        grp = group0 + row_tile // tiles_per_group
        o_ref[...] = x_ref[...] + gate_ref[0, pl.ds(grp, 1), :] * acc
    elif mode == "headnorm":
        (nw_ref,) = extra

        @pl.when(col_tile < n_norm_tiles)
        def _():
            for hh in range(acc.shape[1] // HEAD_DIM):
                cols = slice(hh * HEAD_DIM, (hh + 1) * HEAD_DIM)
                blk = acc[:, cols]
                ms = jnp.mean(blk * blk, axis=-1, keepdims=True)
                o_ref[:, cols] = (blk * lax.rsqrt(ms + EPS) * nw_ref[:, cols]).astype(o_ref.dtype)

        @pl.when(col_tile >= n_norm_tiles)
        def _():
            o_ref[...] = acc.astype(o_ref.dtype)
    else:
        o_ref[...] = acc.astype(o_ref.dtype)


def _proj_kernel(a_ref, w_ref, *rest, j0, **epilogue):
    *extra, o_ref, wb = rest
    j = pl.program_id(0)
    i = pl.program_id(1)

    @pl.when(i == 0)
    def _():
        wb[...] = w_ref[0].astype(BF16)

    acc = jnp.dot(a_ref[...], wb[...], preferred_element_type=F32)
    _proj_epilogue(acc, o_ref, extra, i, j + j0, **epilogue)


def _proj_stream_kernel(a_ref, w_ref, *rest, j0, chunk_rows, **epilogue):
    *extra, o_ref, wb = rest
    jj = pl.program_id(0)
    i = pl.program_id(1)

    def round_chunk():
        r0 = pl.multiple_of(i * chunk_rows, chunk_rows)
        wb[jj % 2, pl.ds(r0, chunk_rows), :] = w_ref[0].astype(BF16)

    @pl.when(jj == 0)
    def _():
        round_chunk()

    @pl.when(jj > 0)
    def _():
        round_chunk()
        acc = jnp.dot(a_ref[...], wb[(jj - 1) % 2], preferred_element_type=F32)
        _proj_epilogue(acc, o_ref, extra, i, jj - 1 + j0, **epilogue)


def _proj(a, w_stack, layer, col0, n_out, out_dtype, *, mode="plain", norm_w=None, n_norm_cols=0,
          x2=None, mod=None, mod_layer=0, gate_col0=0, rows_per_group=None, group0=0):
    m, k = a.shape
    rows_per_group = rows_per_group or m
    tm = min(1024, rows_per_group)
    n_row_tiles = m // tm
    stream = n_row_tiles > 1
    tn = 512 if mode == "resid" or not stream else 1024
    while any(extent % tn for extent in (n_out, col0, n_norm_cols, gate_col0)):
        tn //= 2
    assert tn >= LANES
    n_col_tiles = n_out // tn
    j0 = col0 // tn
    out_bytes = jnp.dtype(out_dtype).itemsize
    block_bytes = 2 * tm * k * 2 + 2 * tm * tn * out_bytes + tm * tn * 4
    epilogue = dict(mode=mode, tiles_per_group=rows_per_group // tm, group0=group0,
                    n_norm_tiles=n_norm_cols // tn)
    if stream:
        assert k % n_row_tiles == 0
        chunk_rows = k // n_row_tiles
        block_bytes += 2 * chunk_rows * tn * 4 + 2 * k * tn * 2
        grid = (n_col_tiles + 1, n_row_tiles)

        def row_of(jj, i):
            return jnp.where(jj == 0, 0, i)

        def col_of(jj):
            return jnp.maximum(jj - 1, 0)

        in_specs = [
            pl.BlockSpec((tm, k), lambda jj, i: (row_of(jj, i), 0)),
            pl.BlockSpec((1, chunk_rows, tn), lambda jj, i: (
                layer, jnp.where(jj == n_col_tiles, 0, i), j0 + jnp.minimum(jj, n_col_tiles - 1))),
        ]
        scratch = pltpu.VMEM((2, k, tn), BF16)
        kern = functools.partial(_proj_stream_kernel, j0=j0, chunk_rows=chunk_rows, **epilogue)
    else:
        block_bytes += 2 * k * tn * 4 + k * tn * 2
        grid = (n_col_tiles, n_row_tiles)

        def row_of(jj, i):
            return i

        def col_of(jj):
            return jj

        in_specs = [
            pl.BlockSpec((tm, k), lambda jj, i: (i, 0)),
            pl.BlockSpec((1, k, tn), lambda jj, i: (layer, 0, j0 + jj)),
        ]
        scratch = pltpu.VMEM((k, tn), BF16)
        kern = functools.partial(_proj_kernel, j0=j0, **epilogue)
    args = [a, w_stack]
    if mode == "headnorm":
        in_specs.append(pl.BlockSpec((1, tn), lambda jj, i: (0, j0 + col_of(jj))))
        args.append(norm_w)
    elif mode == "resid":
        g0 = gate_col0 // tn
        in_specs += [pl.BlockSpec((tm, tn), lambda jj, i: (row_of(jj, i), col_of(jj))),
                     pl.BlockSpec((1, MOD_ROWS, tn), lambda jj, i: (mod_layer, 0, g0 + col_of(jj)))]
        args += [x2, mod]
        block_bytes += 2 * tm * tn * 4
    return pl.pallas_call(
        kern,
        out_shape=jax.ShapeDtypeStruct((m, n_out), out_dtype),
        grid=grid,
        in_specs=in_specs,
        out_specs=pl.BlockSpec((tm, tn), lambda jj, i: (row_of(jj, i), col_of(jj))),
        scratch_shapes=[scratch],
        compiler_params=pltpu.CompilerParams(
            dimension_semantics=("arbitrary", "arbitrary"), vmem_limit_bytes=_vmem_limit(block_bytes)),
        name="proj_" + mode,
    )(*args)


CONV_LEFT = CONV_W // 2
N_SEGMENTS = SUBLANES
SCAN_BLOCK = 8


def _scan_rows(c, tc, seg_len):
    if seg_len >= tc:
        chunks_per_seg = seg_len // tc
        s = lax.div(c, chunks_per_seg)
        k0 = lax.rem(c, chunks_per_seg) * tc
        return [(0, tc, s + N_SEGMENTS * k0, s)]
    segs_per_chunk = tc // seg_len
    return [(e * seg_len, seg_len, c * segs_per_chunk + e, c * segs_per_chunk + e) for e in range(segs_per_chunk)]


def _lru_to_scan_order(src_ref, us_ref, length, tc):
    n_slabs = us_ref.shape[0]
    seg_len = length // N_SEGMENTS
    lead = CONV_LEFT * N_SEGMENTS

    def body(c, carry):
        base = pl.multiple_of(c * tc, tc)
        for row0, n_rows, scan_row0, _ in _scan_rows(c, tc, seg_len):
            dst = pl.ds(lead + scan_row0, n_rows, stride=N_SEGMENTS)
            for slab in range(n_slabs):
                us_ref[slab, dst, :] = src_ref[pl.ds(base + row0, n_rows), slab * LANES:(slab + 1) * LANES]
        return carry

    lax.fori_loop(0, length // tc, body, 0)

    seg = lax.broadcasted_iota(jnp.int32, (N_SEGMENTS, LANES), 0)
    for slab in range(n_slabs):
        def step(k):
            return us_ref[slab, pl.ds(lead + k * N_SEGMENTS, N_SEGMENTS), :]

        for back in range(1, CONV_LEFT + 1):
            halo = jnp.where(seg >= 1, pltpu.roll(step(seg_len - back), 1, 0), 0.0)
            us_ref[slab, pl.ds(lead - back * N_SEGMENTS, N_SEGMENTS), :] = halo
        for ahead in range(CONV_W - 1 - CONV_LEFT):
            halo = jnp.where(seg < N_SEGMENTS - 1, pltpu.roll(step(ahead), N_SEGMENTS - 1, 0), 0.0)
            us_ref[slab, pl.ds(lead + (seg_len + ahead) * N_SEGMENTS, N_SEGMENTS), :] = halo


def _lru_coefficients(us_ref, co_ref, cw_ref, cb_ref, gw_ref, gb_ref, cvec_log2, length, tc):
    n_slabs = us_ref.shape[0]
    width = n_slabs * LANES

    def body(c, carry):
        base = pl.multiple_of(c * tc, tc)
        win = jnp.concatenate([us_ref[slab, pl.ds(base, tc + (CONV_W - 1) * N_SEGMENTS), :]
                               for slab in range(n_slabs)], axis=1)
        u = cw_ref[0:1, :] * win[0:tc]
        for kk in range(1, CONV_W):
            u = u + cw_ref[kk:kk + 1, :] * win[kk * N_SEGMENTS:kk * N_SEGMENTS + tc]
        u = u + cb_ref[...]
        z = jnp.dot(u.astype(BF16), gw_ref[0], preferred_element_type=F32) + gb_ref[0]
        for d in range(2):
            r = _sigmoid(z[:, 2 * d * width:(2 * d + 1) * width])
            i_gate = _sigmoid(z[:, (2 * d + 1) * width:(2 * d + 2) * width])
            a = jnp.exp2(r * cvec_log2[d:d + 1])
            t = 1.0 - a * a
            b = jnp.where(t > 0.0, t * lax.rsqrt(t), 0.0) * (i_gate * u)
            for slab in range(n_slabs):
                lanes = slice(slab * LANES, (slab + 1) * LANES)
                co_ref[2 * d, slab, pl.ds(base, tc), :] = a[:, lanes]
                co_ref[2 * d + 1, slab, pl.ds(base, tc), :] = b[:, lanes]
        return carry

    lax.fori_loop(0, length // tc, body, 0)


def _lru_local_scan(co_ref, length):
    n_slabs = co_ref.shape[1]
    n_blocks = length // N_SEGMENTS // SCAN_BLOCK
    block_rows = SCAN_BLOCK * N_SEGMENTS
    zero = jnp.zeros((N_SEGMENTS, LANES), F32)
    one = jnp.ones((N_SEGMENTS, LANES), F32)

    def body(i, state):
        rows = (pl.ds(pl.multiple_of(i * block_rows, block_rows), block_rows),
                pl.ds(pl.multiple_of((n_blocks - 1 - i) * block_rows, block_rows), block_rows))
        new_state = []
        for d in range(2):
            for slab in range(n_slabs):
                h, p = state[d * n_slabs + slab]
                a_blk = co_ref[2 * d, slab, rows[d], :]
                b_blk = co_ref[2 * d + 1, slab, rows[d], :]
                hs, ps = [None] * SCAN_BLOCK, [None] * SCAN_BLOCK
                for j in (range(SCAN_BLOCK) if d == 0 else range(SCAN_BLOCK - 1, -1, -1)):
                    step = slice(j * N_SEGMENTS, (j + 1) * N_SEGMENTS)
                    h = a_blk[step] * h + b_blk[step]
                    p = a_blk[step] * p
                    hs[j], ps[j] = h, p
                co_ref[2 * d, slab, rows[d], :] = jnp.concatenate(ps, axis=0)
                co_ref[2 * d + 1, slab, rows[d], :] = jnp.concatenate(hs, axis=0)
                new_state.append((h, p))
        return tuple(new_state)

    lax.fori_loop(0, n_blocks, body, tuple((zero, one) for _ in range(2 * n_slabs)))


def _lru_carries(co_ref, carry_ref, length, h0_fwd, h0_bwd):
    n_slabs = co_ref.shape[1]
    seg_len = length // N_SEGMENTS
    ends = (pl.ds((seg_len - 1) * N_SEGMENTS, N_SEGMENTS), pl.ds(0, N_SEGMENTS))
    finals = []
    for d, h0 in enumerate((h0_fwd, h0_bwd)):
        p_end = jnp.concatenate([co_ref[2 * d, slab, ends[d], :] for slab in range(n_slabs)], axis=1)
        h_end = jnp.concatenate([co_ref[2 * d + 1, slab, ends[d], :] for slab in range(n_slabs)], axis=1)
        carry = h0
        for s in (range(N_SEGMENTS) if d == 0 else range(N_SEGMENTS - 1, -1, -1)):
            carry_ref[d, s] = carry
            carry = h_end[s:s + 1] + p_end[s:s + 1] * carry
        finals.append(carry)
    return finals


def _lru_output(co_ref, carry_ref, gate_ref, out_ref, length, tc):
    n_slabs = co_ref.shape[1]
    seg_len = length // N_SEGMENTS

    def body(c, carry):
        base = pl.multiple_of(c * tc, tc)
        for row0, n_rows, scan_row0, seg in _scan_rows(c, tc, seg_len):
            src = pl.ds(scan_row0, n_rows, stride=N_SEGMENTS)
            carries = [carry_ref[d, seg] for d in range(2)]
            slabs = []
            for slab in range(n_slabs):
                lanes = slice(slab * LANES, (slab + 1) * LANES)
                hsum = None
                for d in range(2):
                    h = co_ref[2 * d + 1, slab, src, :] + co_ref[2 * d, slab, src, :] * carries[d][:, lanes]
                    hsum = h if hsum is None else hsum + h
                slabs.append(hsum)
            rows = pl.ds(base + row0, n_rows)
            out_ref[rows, :] = (jnp.concatenate(slabs, axis=1) * _silu(gate_ref[rows, :])).astype(out_ref.dtype)
        return carry

    lax.fori_loop(0, length // tc, body, 0)


def _rglru_kernel(u_ref, g_ref, uc_ref, gc_ref, cw_ref, cb_ref, gw_ref, gb_ref, lam_ref,
                  y_ref, yc_ref, us_l, us_c, co_l, co_c, carry_l, carry_c, *, seq, ctx_len, tc):
    width = u_ref.shape[1]
    lam = lam_ref[0]
    softplus_neg = jnp.maximum(-lam, 0.0) + jnp.log1p(jnp.exp(-jnp.abs(lam)))
    cvec_log2 = (-LRU_C * math.log2(math.e)) * softplus_neg
    params = (cw_ref.at[0], cb_ref.at[0], gw_ref, gb_ref, cvec_log2)

    zeros = jnp.zeros((1, width), F32)
    _lru_to_scan_order(uc_ref, us_c, ctx_len, tc)
    _lru_coefficients(us_c, co_c, *params, ctx_len, tc)
    _lru_local_scan(co_c, ctx_len)
    h_fwd, h_bwd = _lru_carries(co_c, carry_c, ctx_len, zeros, zeros)
    _lru_output(co_c, carry_c, gc_ref, yc_ref, ctx_len, tc)

    _lru_to_scan_order(u_ref, us_l, seq, tc)
    _lru_coefficients(us_l, co_l, *params, seq, tc)
    _lru_local_scan(co_l, seq)
    _lru_carries(co_l, carry_l, seq, h_fwd, h_bwd)
    _lru_output(co_l, carry_l, g_ref, y_ref, seq, tc)


def _rglru(ug_l, ug_c, conv_w, conv_b, gate_w, gate_b, lam, layer, batch, seq, ctx_len):
    d = conv_w.shape[2]
    nb = d // LRU_BLOCK
    w = LRU_BLOCK
    tc = min(256, ctx_len, seq)
    for length in (seq, ctx_len):
        seg_len = length // N_SEGMENTS
        assert length % (N_SEGMENTS * SCAN_BLOCK) == 0 and length % tc == 0
        assert seg_len % tc == 0 or tc % seg_len == 0
    n_slabs = w // LANES
    halo_rows = (CONV_W - 1) * N_SEGMENTS
    gw = jnp.transpose(gate_w[layer], (2, 3, 0, 1, 4)).reshape(nb, w, 4 * w).astype(BF16)
    gb = jnp.transpose(gate_b[layer].reshape(2, 2, nb, w), (2, 0, 1, 3)).reshape(nb, 1, 4 * w)
    block_bytes = (2 * 2 * (seq + ctx_len) * w * 4 + 2 * (seq + ctx_len) * w * 2 + 2 * w * 4 * w * 2
                   + (5 * (seq + ctx_len) + 2 * halo_rows) * w * 4 + 12 * tc * w * 4)
    kern = functools.partial(_rglru_kernel, seq=seq, ctx_len=ctx_len, tc=tc)
    return pl.pallas_call(
        kern,
        out_shape=(jax.ShapeDtypeStruct((batch * seq, d), BF16),
                   jax.ShapeDtypeStruct((batch * ctx_len, d), BF16)),
        grid=(batch, nb),
        in_specs=[
            pl.BlockSpec((seq, w), lambda b, n: (b, n)),
            pl.BlockSpec((seq, w), lambda b, n: (b, nb + n)),
            pl.BlockSpec((ctx_len, w), lambda b, n: (b, n)),
            pl.BlockSpec((ctx_len, w), lambda b, n: (b, nb + n)),
            pl.BlockSpec((1, CONV_W, w), lambda b, n: (layer, 0, n)),
            pl.BlockSpec((1, 1, w), lambda b, n: (layer, 0, n)),
            pl.BlockSpec((1, w, 4 * w), lambda b, n: (n, 0, 0)),
            pl.BlockSpec((1, 1, 4 * w), lambda b, n: (n, 0, 0)),
            pl.BlockSpec((1, 2, w), lambda b, n: (layer, 0, n)),
        ],
        out_specs=(pl.BlockSpec((seq, w), lambda b, n: (b, n)),
                   pl.BlockSpec((ctx_len, w), lambda b, n: (b, n))),
        scratch_shapes=[
            pltpu.VMEM((n_slabs, seq + halo_rows, LANES), F32),
            pltpu.VMEM((n_slabs, ctx_len + halo_rows, LANES), F32),
            pltpu.VMEM((4, n_slabs, seq, LANES), F32),
            pltpu.VMEM((4, n_slabs, ctx_len, LANES), F32),
            pltpu.VMEM((2, N_SEGMENTS, 1, w), F32),
            pltpu.VMEM((2, N_SEGMENTS, 1, w), F32),
        ],
        compiler_params=pltpu.CompilerParams(
            dimension_semantics=("parallel", "parallel"), vmem_limit_bytes=_vmem_limit(block_bytes)),
        name="rglru",
    )(ug_l, ug_l, ug_c, ug_c, conv_w, conv_b, gw, gb, lam)


NA_QROWS = 4
NA_WIN_ROWS = 12
assert NA_WIN_ROWS >= NA_QROWS + NA_KH - 1 and (NA_WIN_ROWS * GRID_W) % MXU_DIM == 0
NA_BLOCKS_PER_STEP = 2
NA_TILE_ROWS = LANES // GRID_W
assert NA_TILE_ROWS == 2 and NA_WIN_ROWS % NA_TILE_ROWS == 0
N_BIAS_ROWS = 2 * NA_KH - 1
BIAS_BLOCKS = N_BIAS_ROWS + 3


def _nt_dot(a, b):
    return lax.dot_general(a, b, (((1,), (1,)), ((), ())), preferred_element_type=F32)


def _softmax_pv(scores, values):
    m = scores[0].max(axis=-1, keepdims=True)
    for s in scores[1:]:
        m = jnp.maximum(m, s.max(axis=-1, keepdims=True))
    ps = [jnp.exp(s - m) for s in scores]
    denom = ps[0].sum(axis=-1, keepdims=True)
    for p in ps[1:]:
        denom = denom + p.sum(axis=-1, keepdims=True)
    o = jnp.dot(ps[0].astype(BF16), values[0], preferred_element_type=F32)
    for p, v in zip(ps[1:], values[1:]):
        o = o + jnp.dot(p.astype(BF16), v, preferred_element_type=F32)
    return o / denom


def _tile_plan(rows):
    n_blocks = rows // NA_QROWS
    plans = {}
    for blk in range(n_blocks):
        kind = "first" if blk == 0 else ("last" if blk == n_blocks - 1 else "inner")
        win_start = min(max(blk * NA_QROWS - NA_KH // 2, 0), rows - NA_WIN_ROWS)
        plan = []
        for qi in range(NA_QROWS):
            r = blk * NA_QROWS + qi
            row_start = min(max(r - NA_KH // 2, 0), rows - NA_KH)
            assert win_start <= row_start and row_start + NA_KH <= win_start + NA_WIN_ROWS
            tiles = []
            for t in range(NA_WIN_ROWS // NA_TILE_ROWS):
                key_rows = [win_start + t * NA_TILE_ROWS + e for e in range(NA_TILE_ROWS)]
                inside = [row_start <= kr < row_start + NA_KH for kr in key_rows]
                if not any(inside):
                    tiles.append(None)
                    continue
                e0 = inside.index(True)
                block = 1 + (key_rows[e0] - r + NA_KH - 1) - e0
                assert 0 <= block and block + NA_TILE_ROWS <= BIAS_BLOCKS - 1
                keep = "both" if all(inside) else ("left" if inside[0] else "right")
                tiles.append((block % 2, (block - block % 2) * GRID_W, keep))
            plan.append(tuple(tiles))
        assert plans.setdefault(kind, tuple(plan)) == tuple(plan)
    return plans


def _attend_block(q, kw, vw, kc, vc, bias_ref, plan):
    s_lat = _nt_dot(q, kw)
    s_ctx = _nt_dot(q, kc)
    lane = lax.broadcasted_iota(jnp.int32, (GRID_W, LANES), 1)
    p_lat_rows, p_ctx_rows, denoms = [], [], []
    for qi, tiles in enumerate(plan):
        rs = slice(qi * GRID_W, (qi + 1) * GRID_W)
        scored = []
        for t, spec in enumerate(tiles):
            if spec is None:
                scored.append(None)
                continue
            copy, off, keep = spec
            s = s_lat[rs, t * LANES:(t + 1) * LANES] + bias_ref[0, copy, :, off:off + LANES]
            if keep == "left":
                s = jnp.where(lane < GRID_W, s, MASK_VALUE)
            elif keep == "right":
                s = jnp.where(lane >= GRID_W, s, MASK_VALUE)
            scored.append(s)
        ctx_tiles = [s_ctx[rs, t * LANES:(t + 1) * LANES] for t in range(s_ctx.shape[1] // LANES)]
        live = [s for s in scored if s is not None] + ctx_tiles
        m = functools.reduce(jnp.maximum, live).max(axis=-1, keepdims=True)
        p_lat = [None if s is None else jnp.exp(s - m) for s in scored]
        p_ctx = [jnp.exp(s - m) for s in ctx_tiles]
        total = functools.reduce(jnp.add, [p for p in p_lat if p is not None] + p_ctx)
        denoms.append(total.sum(axis=-1, keepdims=True))
        zero = jnp.zeros((GRID_W, LANES), BF16)
        p_lat_rows.append(jnp.concatenate([zero if p is None else p.astype(BF16) for p in p_lat], axis=1))
        p_ctx_rows.append(jnp.concatenate([p.astype(BF16) for p in p_ctx], axis=1))
    o = jnp.dot(jnp.concatenate(p_lat_rows, axis=0), vw, preferred_element_type=F32)
    o = o + jnp.dot(jnp.concatenate(p_ctx_rows, axis=0), vc, preferred_element_type=F32)
    return o / jnp.concatenate(denoms, axis=0)


def _natten_kernel(*refs, rows, need_ctx):
    if need_ctx:
        q_ref, k_ref, v_ref, g_ref, kc_ref, vc_ref, bias_ref, qc_ref, gc_ref, y_ref, yc_ref = refs
    else:
        q_ref, k_ref, v_ref, g_ref, kc_ref, vc_ref, bias_ref, y_ref = refs
    kc = kc_ref[...]
    vc = vc_ref[...]
    n_blocks = rows // NA_QROWS
    nq = NA_QROWS * GRID_W
    nk = NA_WIN_ROWS * GRID_W
    plans = _tile_plan(rows)

    def attend(blk, kind):
        if isinstance(blk, int):
            q0 = blk * nq
            k0 = min(max(blk * NA_QROWS - NA_KH // 2, 0), rows - NA_WIN_ROWS) * GRID_W
        else:
            q0 = pl.multiple_of(blk * nq, nq)
            k0 = pl.multiple_of(jnp.clip(blk * NA_QROWS - NA_KH // 2, 0, rows - NA_WIN_ROWS) * GRID_W, GRID_W)
        o = _attend_block(q_ref[pl.ds(q0, nq), :], k_ref[pl.ds(k0, nk), :], v_ref[pl.ds(k0, nk), :],
                          kc, vc, bias_ref, plans[kind])
        y_ref[pl.ds(q0, nq), :] = (o * _silu(g_ref[pl.ds(q0, nq), :])).astype(y_ref.dtype)

    def inner(it, carry):
        for sub in range(NA_BLOCKS_PER_STEP):
            attend(1 + it * NA_BLOCKS_PER_STEP + sub, "inner")
        return carry

    attend(0, "first")
    lax.fori_loop(0, (n_blocks - 2) // NA_BLOCKS_PER_STEP, inner, 0)
    attend(n_blocks - 1, "last")

    if need_ctx:
        oc = _softmax_pv([_nt_dot(qc_ref[...], kc)], [vc])
        yc_ref[...] = (oc * _silu(gc_ref[...])).astype(yc_ref.dtype)


def _bias_table(rpb):
    n_heads = rpb.shape[0]
    qc = np.arange(GRID_W)[:, None]
    kcol = np.arange(GRID_W)[None, :]
    col_start = np.clip(qc - NA_KW // 2, 0, GRID_W - NA_KW)
    inside = (kcol >= col_start) & (kcol < col_start + NA_KW)
    col_idx = np.clip(kcol - qc, -(NA_KW - 1), NA_KW - 1) + (NA_KW - 1)
    onehot = (col_idx[:, :, None] == np.arange(2 * NA_KW - 1)) & inside[:, :, None]
    tab = jnp.einsum('ckm,hjm->hcjk', jnp.asarray(onehot, F32), rpb, precision=lax.Precision.HIGHEST)
    tab = tab + jnp.asarray(np.where(inside, 0.0, MASK_VALUE), F32)[None, :, None, :]
    tab = tab.reshape(n_heads, GRID_W, N_BIAS_ROWS * GRID_W)

    def padded(left_blocks):
        right_blocks = BIAS_BLOCKS - N_BIAS_ROWS - left_blocks
        return jnp.pad(tab, ((0, 0), (0, 0), (left_blocks * GRID_W, right_blocks * GRID_W)),
                       constant_values=MASK_VALUE)

    return jnp.stack([padded(1), padded(0)], axis=1)


def _natten(qkv_l, g_l, kv_c, kv_c_head0, bias, batch, seq, ctx_len, d, g_c=None):
    n_heads = d // HEAD_DIM
    rows = seq // GRID_W
    need_ctx = g_c is not None
    hd = HEAD_DIM
    in_specs = [
        pl.BlockSpec((seq, hd), lambda b, h: (b, h)),
        pl.BlockSpec((seq, hd), lambda b, h: (b, n_heads + h)),
        pl.BlockSpec((seq, hd), lambda b, h: (b, 2 * n_heads + h)),
        pl.BlockSpec((seq, hd), lambda b, h: (b, h)),
        pl.BlockSpec((ctx_len, hd), lambda b, h: (b, kv_c_head0 + h)),
        pl.BlockSpec((ctx_len, hd), lambda b, h: (b, kv_c_head0 + n_heads + h)),
        pl.BlockSpec((1,) + bias.shape[1:], lambda b, h: (h, 0, 0, 0)),
    ]
    args = [qkv_l, qkv_l, qkv_l, g_l, kv_c, kv_c, bias]
    out_shape = [jax.ShapeDtypeStruct((batch * seq, d), BF16)]
    out_specs = [pl.BlockSpec((seq, hd), lambda b, h: (b, h))]
    if need_ctx:
        in_specs += [pl.BlockSpec((ctx_len, hd), lambda b, h: (b, h)),
                     pl.BlockSpec((ctx_len, hd), lambda b, h: (b, h))]
        args += [kv_c, g_c]
        out_shape.append(jax.ShapeDtypeStruct((batch * ctx_len, d), BF16))
        out_specs.append(pl.BlockSpec((ctx_len, hd), lambda b, h: (b, h)))
    nq, nk = NA_QROWS * GRID_W, NA_WIN_ROWS * GRID_W
    block_bytes = (2 * (4 * seq * hd * 2 + seq * hd * 4 + 4 * ctx_len * hd * 4)
                   + 2 * bias[0].size * 4 + 4 * NA_BLOCKS_PER_STEP * nq * (nk + ctx_len) * 4)
    kern = functools.partial(_natten_kernel, rows=rows, need_ctx=need_ctx)
    return pl.pallas_call(
        kern,
        out_shape=tuple(out_shape),
        grid=(batch, n_heads),
        in_specs=in_specs,
        out_specs=tuple(out_specs),
        compiler_params=pltpu.CompilerParams(
            dimension_semantics=("parallel", "parallel"), vmem_limit_bytes=_vmem_limit(block_bytes)),
        name="natten",
    )(*args)


def kernel(x, c, ctx, c_ctx, ada_w, ada_b, norm_g, lru_in_w, lru_conv_w, lru_conv_b, lru_gate_w,
           lru_gate_b, lru_lambda, lru_out_w, na_in_w, na_qk_norm, na_rpb, na_out_w):
    batch, seq, d = x.shape
    ctx_len = ctx.shape[1]
    depth = ada_w.shape[0]
    n_heads = d // HEAD_DIM
    rows = seq // GRID_W
    assert batch + 1 <= MOD_ROWS and seq % GRID_W == 0
    assert rows % NA_QROWS == 0 and rows >= NA_WIN_ROWS + NA_QROWS
    assert (rows // NA_QROWS - 2) % NA_BLOCKS_PER_STEP == 0 and ctx_len % LANES == 0

    xl = x.reshape(batch * seq, d)
    xc = ctx.reshape(batch * ctx_len, d)
    cvec = jnp.zeros((MOD_ROWS, d), F32).at[:batch].set(c).at[batch].set(c_ctx)
    mod = _adaln(cvec, ada_w, ada_b)
    lat = dict(rows_per_group=seq, group0=0)
    con = dict(rows_per_group=batch * ctx_len, group0=batch)
    conv_b = lru_conv_b.reshape(lru_conv_b.shape[0], 1, d)

    for i in range(depth):
        need_ctx = i < depth - 1
        j = i // 2
        hl = _norm_mod(xl, norm_g, mod, i, **lat)
        hc = _norm_mod(xc, norm_g, mod, i, **con)
        if i % 2 == 0:
            assert need_ctx
            ug_l = _proj(hl, lru_in_w, j, 0, 2 * d, F32)
            ug_c = _proj(hc, lru_in_w, j, 0, 2 * d, F32)
            y_l, y_c = _rglru(ug_l, ug_c, lru_conv_w, conv_b, lru_gate_w, lru_gate_b, lru_lambda,
                              j, batch, seq, ctx_len)
            w_out = lru_out_w
        else:
            norm_w = jnp.concatenate([jnp.tile(na_qk_norm[j, 0] * ATTN_SCALE, n_heads),
                                      jnp.tile(na_qk_norm[j, 1], n_heads),
                                      jnp.ones((d,), F32)]).reshape(1, 3 * d)
            bias = _bias_table(na_rpb[j])
            qkv = dict(mode="headnorm", norm_w=norm_w, n_norm_cols=2 * d)
            qkv_l = _proj(hl, na_in_w, j, 0, 3 * d, BF16, **qkv)
            g_l = _proj(hl, na_in_w, j, 3 * d, d, F32)
            if need_ctx:
                qkv_c = _proj(hc, na_in_w, j, 0, 3 * d, BF16, **qkv)
                g_c = _proj(hc, na_in_w, j, 3 * d, d, F32)
                y_l, y_c = _natten(qkv_l, g_l, qkv_c, n_heads, bias, batch, seq, ctx_len, d, g_c)
            else:
                kv_c = _proj(hc, na_in_w, j, d, 2 * d, BF16, **qkv)
                (y_l,) = _natten(qkv_l, g_l, kv_c, 0, bias, batch, seq, ctx_len, d)
                y_c = None
            w_out = na_out_w
        res = dict(mode="resid", mod=mod, mod_layer=i, gate_col0=2 * d)
        xl = _proj(y_l, w_out, j, 0, d, F32, x2=xl, **res, **lat)
        if need_ctx:
            xc = _proj(y_c, w_out, j, 0, d, F32, x2=xc, **res, **con)
    return xl.reshape(batch, seq, d)
```

```python
import functools
import math

import numpy as np
import jax
import jax.numpy as jnp
from jax import lax
from jax.experimental import pallas as pl
from jax.experimental.pallas import tpu as pltpu

GRID_W = 64
NA_KH = 8
NA_KW = 16
HEAD_DIM = 128
LRU_BLOCK = 256
CONV_W = 4
LRU_C = 8.0
EPS = 1e-6
ATTN_SCALE = HEAD_DIM ** -0.5
MASK_VALUE = -1e30

V7X_VMEM_BYTES = 64 * 1024 * 1024
SUBLANES = 8
LANES = 128
MXU_DIM = 256
MOD_ROWS = 8

F32 = jnp.float32
BF16 = jnp.bfloat16


def _vmem_limit(block_bytes):
    return int(min(block_bytes + (16 << 20), V7X_VMEM_BYTES - (4 << 20)))


def _silu(v):
    half = 0.5 * v
    return half * jnp.tanh(half) + half


def _adaln_kernel(c_ref, w_ref, b_ref, o_ref):
    cond = _silu(c_ref[...]).astype(BF16)
    w = w_ref[0].astype(BF16)
    o_ref[0] = jnp.dot(cond, w, preferred_element_type=F32) + b_ref[0]


def _adaln(cvec, ada_w, ada_b):
    n_layers, d, n3 = ada_w.shape
    tn = min(512, n3)
    block_bytes = 2 * d * tn * 4 + d * tn * 2 + 4 * MOD_ROWS * (d + 2 * tn) * 4
    return pl.pallas_call(
        _adaln_kernel,
        out_shape=jax.ShapeDtypeStruct((n_layers, MOD_ROWS, n3), F32),
        grid=(n_layers, n3 // tn),
        in_specs=[
            pl.BlockSpec((MOD_ROWS, d), lambda l, j: (0, 0)),
            pl.BlockSpec((1, d, tn), lambda l, j: (l, 0, j)),
            pl.BlockSpec((1, 1, tn), lambda l, j: (l, 0, j)),
        ],
        out_specs=pl.BlockSpec((1, MOD_ROWS, tn), lambda l, j: (l, 0, j)),
        compiler_params=pltpu.CompilerParams(
            dimension_semantics=("parallel", "parallel"), vmem_limit_bytes=_vmem_limit(block_bytes)),
        name="adaln",
    )(cvec, ada_w, ada_b.reshape(n_layers, 1, n3))


def _norm_kernel(x_ref, g_ref, mod_ref, o_ref, *, tiles_per_group, group0, d):
    grp = group0 + pl.program_id(0) // tiles_per_group
    x = x_ref[...]
    ms = jnp.mean(x * x, axis=-1, keepdims=True)
    y = x * lax.rsqrt(ms + EPS) * g_ref[0]
    shift = mod_ref[0, pl.ds(grp, 1), 0:d]
    scale = mod_ref[0, pl.ds(grp, 1), d:2 * d]
    o_ref[...] = (y * (1.0 + scale) + shift).astype(BF16)


def _norm_mod(x2, norm_g, mod, layer, rows_per_group, group0):
    m, d = x2.shape
    tm = min(256, rows_per_group)
    block_bytes = 2 * tm * d * 4 + 2 * tm * d * 2 + 2 * MOD_ROWS * 3 * d * 4 + 3 * tm * d * 4
    kern = functools.partial(_norm_kernel, tiles_per_group=rows_per_group // tm, group0=group0, d=d)
    return pl.pallas_call(
        kern,
        out_shape=jax.ShapeDtypeStruct((m, d), BF16),
        grid=(m // tm,),
        in_specs=[
            pl.BlockSpec((tm, d), lambda i: (i, 0)),
            pl.BlockSpec((1, 1, d), lambda i: (layer, 0, 0)),
            pl.BlockSpec((1, MOD_ROWS, 3 * d), lambda i: (layer, 0, 0)),
        ],
        out_specs=pl.BlockSpec((tm, d), lambda i: (i, 0)),
        compiler_params=pltpu.CompilerParams(
            dimension_semantics=("parallel",), vmem_limit_bytes=_vmem_limit(block_bytes)),
        name="norm_mod",
    )(x2, norm_g.reshape(norm_g.shape[0], 1, d), mod)


PROJ_ROW_BLOCK = 256


def _proj_tile(a_ref, w, o_ref, extra, row_tile, col_tile, *, mode, tiles_per_group, group0, n_norm_tiles):
    tm = a_ref.shape[0]
    row_block = min(PROJ_ROW_BLOCK, tm)

    def blocks(epilogue):
        for r0 in range(0, tm, row_block):
            rows = slice(r0, r0 + row_block)
            epilogue(rows, jnp.dot(a_ref[rows, :], w, preferred_element_type=F32))

    def plain(rows, acc):
        o_ref[rows, :] = acc.astype(o_ref.dtype)

    if mode == "resid":
        x_ref, gate_ref = extra
        gate = gate_ref[0, pl.ds(group0 + row_tile // tiles_per_group, 1), :]

        def resid(rows, acc):
            o_ref[rows, :] = x_ref[rows, :] + gate * acc

        blocks(resid)
    elif mode == "headnorm":
        (nw_ref,) = extra

        def headnorm(rows, acc):
            for hh in range(acc.shape[1] // HEAD_DIM):
                cols = slice(hh * HEAD_DIM, (hh + 1) * HEAD_DIM)
                blk = acc[:, cols]
                ms = jnp.mean(blk * blk, axis=-1, keepdims=True)
                o_ref[rows, cols] = (blk * lax.rsqrt(ms + EPS) * nw_ref[:, cols]).astype(o_ref.dtype)

        pl.when(col_tile < n_norm_tiles)(lambda: blocks(headnorm))
        pl.when(col_tile >= n_norm_tiles)(lambda: blocks(plain))
    else:
        blocks(plain)


def _proj_kernel(a_ref, w_ref, *rest, j0, **epilogue):
    *extra, o_ref, wb = rest
    j = pl.program_id(0)
    i = pl.program_id(1)

    @pl.when(i == 0)
    def _():
        wb[...] = w_ref[0].astype(BF16)

    _proj_tile(a_ref, wb[...], o_ref, extra, i, j + j0, **epilogue)


def _proj_stream_kernel(a_ref, w_ref, *rest, j0, chunk_rows, **epilogue):
    *extra, o_ref, wb = rest
    jj = pl.program_id(0)
    i = pl.program_id(1)

    def round_chunk():
        r0 = pl.multiple_of(i * chunk_rows, chunk_rows)
        wb[jj % 2, pl.ds(r0, chunk_rows), :] = w_ref[0].astype(BF16)

    @pl.when(jj == 0)
    def _():
        round_chunk()

    @pl.when(jj > 0)
    def _():
        round_chunk()
        _proj_tile(a_ref, wb[(jj - 1) % 2], o_ref, extra, i, jj - 1 + j0, **epilogue)


def _proj(a, w_stack, layer, col0, n_out, out_dtype, *, mode="plain", norm_w=None, n_norm_cols=0,
          x2=None, mod=None, mod_layer=0, gate_col0=0, rows_per_group=None, group0=0):
    m, k = a.shape
    rows_per_group = rows_per_group or m
    tm = min(1024, rows_per_group)
    n_row_tiles = m // tm
    stream = n_row_tiles > 1
    tn = 512 if mode == "resid" or not stream else 1024
    while any(extent % tn for extent in (n_out, col0, n_norm_cols, gate_col0)):
        tn //= 2
    assert tn >= LANES
    n_col_tiles = n_out // tn
    j0 = col0 // tn
    out_bytes = jnp.dtype(out_dtype).itemsize
    block_bytes = 2 * tm * k * 2 + 2 * tm * tn * out_bytes + tm * tn * 4
    epilogue = dict(mode=mode, tiles_per_group=rows_per_group // tm, group0=group0,
                    n_norm_tiles=n_norm_cols // tn)
    if stream:
        assert k % n_row_tiles == 0
        chunk_rows = k // n_row_tiles
        block_bytes += 2 * chunk_rows * tn * 4 + 2 * k * tn * 2
        grid = (n_col_tiles + 1, n_row_tiles)

        def row_of(jj, i):
            return jnp.where(jj == 0, 0, i)

        def col_of(jj):
            return jnp.maximum(jj - 1, 0)

        in_specs = [
            pl.BlockSpec((tm, k), lambda jj, i: (row_of(jj, i), 0)),
            pl.BlockSpec((1, chunk_rows, tn), lambda jj, i: (
                layer, jnp.where(jj == n_col_tiles, 0, i), j0 + jnp.minimum(jj, n_col_tiles - 1))),
        ]
        scratch = pltpu.VMEM((2, k, tn), BF16)
        kern = functools.partial(_proj_stream_kernel, j0=j0, chunk_rows=chunk_rows, **epilogue)
    else:
        block_bytes += 2 * k * tn * 4 + k * tn * 2
        grid = (n_col_tiles, n_row_tiles)

        def row_of(jj, i):
            return i

        def col_of(jj):
            return jj

        in_specs = [
            pl.BlockSpec((tm, k), lambda jj, i: (i, 0)),
            pl.BlockSpec((1, k, tn), lambda jj, i: (layer, 0, j0 + jj)),
        ]
        scratch = pltpu.VMEM((k, tn), BF16)
        kern = functools.partial(_proj_kernel, j0=j0, **epilogue)
    args = [a, w_stack]
    if mode == "headnorm":
        in_specs.append(pl.BlockSpec((1, tn), lambda jj, i: (0, j0 + col_of(jj))))
        args.append(norm_w)
    elif mode == "resid":
        g0 = gate_col0 // tn
        in_specs += [pl.BlockSpec((tm, tn), lambda jj, i: (row_of(jj, i), col_of(jj))),
                     pl.BlockSpec((1, MOD_ROWS, tn), lambda jj, i: (mod_layer, 0, g0 + col_of(jj)))]
        args += [x2, mod]
        block_bytes += 2 * tm * tn * 4
    return pl.pallas_call(
        kern,
        out_shape=jax.ShapeDtypeStruct((m, n_out), out_dtype),
        grid=grid,
        in_specs=in_specs,
        out_specs=pl.BlockSpec((tm, tn), lambda jj, i: (row_of(jj, i), col_of(jj))),
        scratch_shapes=[scratch],
        compiler_params=pltpu.CompilerParams(
            dimension_semantics=("arbitrary", "arbitrary"), vmem_limit_bytes=_vmem_limit(block_bytes)),
        name="proj_" + mode,
    )(*args)


CONV_LEFT = CONV_W // 2
N_SEGMENTS = SUBLANES
SCAN_BLOCK = 8


def _scan_rows(c, tc, seg_len):
    if seg_len >= tc:
        chunks_per_seg = seg_len // tc
        s = lax.div(c, chunks_per_seg)
        k0 = lax.rem(c, chunks_per_seg) * tc
        return [(0, tc, s + N_SEGMENTS * k0, s)]
    segs_per_chunk = tc // seg_len
    return [(e * seg_len, seg_len, c * segs_per_chunk + e, c * segs_per_chunk + e) for e in range(segs_per_chunk)]


def _lru_to_scan_order(src_ref, us_ref, length, tc):
    n_slabs = us_ref.shape[0]
    seg_len = length // N_SEGMENTS
    lead = CONV_LEFT * N_SEGMENTS

    def body(c, carry):
        base = pl.multiple_of(c * tc, tc)
        for row0, n_rows, scan_row0, _ in _scan_rows(c, tc, seg_len):
            dst = pl.ds(lead + scan_row0, n_rows, stride=N_SEGMENTS)
            for slab in range(n_slabs):
                us_ref[slab, dst, :] = src_ref[pl.ds(base + row0, n_rows), slab * LANES:(slab + 1) * LANES]
        return carry

    lax.fori_loop(0, length // tc, body, 0)

    seg = lax.broadcasted_iota(jnp.int32, (N_SEGMENTS, LANES), 0)
    for slab in range(n_slabs):
        def step(k):
            return us_ref[slab, pl.ds(lead + k * N_SEGMENTS, N_SEGMENTS), :]

        for back in range(1, CONV_LEFT + 1):
            halo = jnp.where(seg >= 1, pltpu.roll(step(seg_len - back), 1, 0), 0.0)
            us_ref[slab, pl.ds(lead - back * N_SEGMENTS, N_SEGMENTS), :] = halo
        for ahead in range(CONV_W - 1 - CONV_LEFT):
            halo = jnp.where(seg < N_SEGMENTS - 1, pltpu.roll(step(ahead), N_SEGMENTS - 1, 0), 0.0)
            us_ref[slab, pl.ds(lead + (seg_len + ahead) * N_SEGMENTS, N_SEGMENTS), :] = halo


def _lru_coefficients(us_ref, co_ref, cw_ref, cb_ref, gw_ref, gb_ref, cvec_log2, length, tc):
    n_slabs = us_ref.shape[0]
    width = n_slabs * LANES

    def body(c, carry):
        base = pl.multiple_of(c * tc, tc)
        win = jnp.concatenate([us_ref[slab, pl.ds(base, tc + (CONV_W - 1) * N_SEGMENTS), :]
                               for slab in range(n_slabs)], axis=1)
        u = cw_ref[0:1, :] * win[0:tc]
        for kk in range(1, CONV_W):
            u = u + cw_ref[kk:kk + 1, :] * win[kk * N_SEGMENTS:kk * N_SEGMENTS + tc]
        u = u + cb_ref[...]
        zh = jnp.dot(u.astype(BF16), gw_ref[0], preferred_element_type=F32) + gb_ref[0]
        for d in range(2):
            r = 0.5 * jnp.tanh(zh[:, 2 * d * width:(2 * d + 1) * width]) + 0.5
            i_gate = 0.5 * jnp.tanh(zh[:, (2 * d + 1) * width:(2 * d + 2) * width]) + 0.5
            a = jnp.exp2(r * cvec_log2[d:d + 1])
            t = 1.0 - a * a
            b = jnp.where(t > 0.0, t * lax.rsqrt(t), 0.0) * (i_gate * u)
            for slab in range(n_slabs):
                lanes = slice(slab * LANES, (slab + 1) * LANES)
                co_ref[2 * d, slab, pl.ds(base, tc), :] = a[:, lanes]
                co_ref[2 * d + 1, slab, pl.ds(base, tc), :] = b[:, lanes]
        return carry

    lax.fori_loop(0, length // tc, body, 0)


def _lru_local_scan(co_ref, length):
    n_slabs = co_ref.shape[1]
    n_blocks = length // N_SEGMENTS // SCAN_BLOCK
    block_rows = SCAN_BLOCK * N_SEGMENTS
    zero = jnp.zeros((N_SEGMENTS, LANES), F32)
    one = jnp.ones((N_SEGMENTS, LANES), F32)

    def body(i, state):
        rows = (pl.ds(pl.multiple_of(i * block_rows, block_rows), block_rows),
                pl.ds(pl.multiple_of((n_blocks - 1 - i) * block_rows, block_rows), block_rows))
        new_state = []
        for d in range(2):
            for slab in range(n_slabs):
                h, p = state[d * n_slabs + slab]
                a_blk = co_ref[2 * d, slab, rows[d], :]
                b_blk = co_ref[2 * d + 1, slab, rows[d], :]
                hs, ps = [None] * SCAN_BLOCK, [None] * SCAN_BLOCK
                for j in (range(SCAN_BLOCK) if d == 0 else range(SCAN_BLOCK - 1, -1, -1)):
                    step = slice(j * N_SEGMENTS, (j + 1) * N_SEGMENTS)
                    h = a_blk[step] * h + b_blk[step]
                    p = a_blk[step] * p
                    hs[j], ps[j] = h, p
                co_ref[2 * d, slab, rows[d], :] = jnp.concatenate(ps, axis=0)
                co_ref[2 * d + 1, slab, rows[d], :] = jnp.concatenate(hs, axis=0)
                new_state.append((h, p))
        return tuple(new_state)

    lax.fori_loop(0, n_blocks, body, tuple((zero, one) for _ in range(2 * n_slabs)))


def _lru_carries(co_ref, carry_ref, length, h0_fwd, h0_bwd):
    n_slabs = co_ref.shape[1]
    seg_len = length // N_SEGMENTS
    ends = (pl.ds((seg_len - 1) * N_SEGMENTS, N_SEGMENTS), pl.ds(0, N_SEGMENTS))
    finals = []
    for d, h0 in enumerate((h0_fwd, h0_bwd)):
        p_end = jnp.concatenate([co_ref[2 * d, slab, ends[d], :] for slab in range(n_slabs)], axis=1)
        h_end = jnp.concatenate([co_ref[2 * d + 1, slab, ends[d], :] for slab in range(n_slabs)], axis=1)
        carry = h0
        for s in (range(N_SEGMENTS) if d == 0 else range(N_SEGMENTS - 1, -1, -1)):
            carry_ref[d, pl.ds(s, 1), :] = carry
            carry = h_end[s:s + 1] + p_end[s:s + 1] * carry
        finals.append(carry)
    return finals


def _lru_output(co_ref, carry_ref, gate_ref, out_ref, length, tc):
    n_slabs = co_ref.shape[1]
    seg_len = length // N_SEGMENTS
    steps = tc // N_SEGMENTS

    def combine(c, carry):
        rows = pl.ds(pl.multiple_of(c * tc, tc), tc)
        for slab in range(n_slabs):
            lanes = slice(slab * LANES, (slab + 1) * LANES)
            hsum = None
            for d in range(2):
                cin = carry_ref[d, :, lanes][None]
                p = co_ref[2 * d, slab, rows, :].reshape(steps, N_SEGMENTS, LANES)
                h = co_ref[2 * d + 1, slab, rows, :].reshape(steps, N_SEGMENTS, LANES) + p * cin
                hsum = h if hsum is None else hsum + h
            co_ref[1, slab, rows, :] = hsum.reshape(tc, LANES)
        return carry

    lax.fori_loop(0, length // tc, combine, 0)

    def emit(c, carry):
        base = pl.multiple_of(c * tc, tc)
        for row0, n_rows, scan_row0, _ in _scan_rows(c, tc, seg_len):
            src = pl.ds(scan_row0, n_rows, stride=N_SEGMENTS)
            hsum = jnp.concatenate([co_ref[1, slab, src, :] for slab in range(n_slabs)], axis=1)
            rows = pl.ds(base + row0, n_rows)
            out_ref[rows, :] = (hsum * _silu(gate_ref[rows, :])).astype(out_ref.dtype)
        return carry

    lax.fori_loop(0, length // tc, emit, 0)


def _rglru_kernel(u_ref, g_ref, uc_ref, gc_ref, cw_ref, cb_ref, gw_ref, gb_ref, lam_ref,
                  y_ref, yc_ref, us_l, us_c, co_l, co_c, carry_l, carry_c, *, seq, ctx_len, tc):
    width = u_ref.shape[1]
    lam = lam_ref[0]
    softplus_neg = jnp.maximum(-lam, 0.0) + jnp.log1p(jnp.exp(-jnp.abs(lam)))
    cvec_log2 = (-LRU_C * math.log2(math.e)) * softplus_neg
    params = (cw_ref.at[0], cb_ref.at[0], gw_ref, gb_ref, cvec_log2)

    zeros = jnp.zeros((1, width), F32)
    _lru_to_scan_order(uc_ref, us_c, ctx_len, tc)
    _lru_coefficients(us_c, co_c, *params, ctx_len, tc)
    _lru_local_scan(co_c, ctx_len)
    h_fwd, h_bwd = _lru_carries(co_c, carry_c, ctx_len, zeros, zeros)
    _lru_output(co_c, carry_c, gc_ref, yc_ref, ctx_len, tc)

    _lru_to_scan_order(u_ref, us_l, seq, tc)
    _lru_coefficients(us_l, co_l, *params, seq, tc)
    _lru_local_scan(co_l, seq)
    _lru_carries(co_l, carry_l, seq, h_fwd, h_bwd)
    _lru_output(co_l, carry_l, g_ref, y_ref, seq, tc)


def _rglru(ug_l, ug_c, conv_w, conv_b, gate_w, gate_b, lam, layer, batch, seq, ctx_len):
    d = conv_w.shape[2]
    nb = d // LRU_BLOCK
    w = LRU_BLOCK
    tc = min(256, ctx_len, seq)
    for length in (seq, ctx_len):
        seg_len = length // N_SEGMENTS
        assert length % (N_SEGMENTS * SCAN_BLOCK) == 0 and length % tc == 0
        assert seg_len % tc == 0 or tc % seg_len == 0
    n_slabs = w // LANES
    halo_rows = (CONV_W - 1) * N_SEGMENTS
    gw = (0.5 * jnp.transpose(gate_w[layer], (2, 3, 0, 1, 4))).reshape(nb, w, 4 * w).astype(BF16)
    gb = 0.5 * jnp.transpose(gate_b[layer].reshape(2, 2, nb, w), (2, 0, 1, 3)).reshape(nb, 1, 4 * w)
    block_bytes = (2 * 2 * (seq + ctx_len) * w * 4 + 2 * (seq + ctx_len) * w * 2 + 2 * w * 4 * w * 2
                   + (5 * (seq + ctx_len) + 2 * halo_rows) * w * 4 + 12 * tc * w * 4)
    kern = functools.partial(_rglru_kernel, seq=seq, ctx_len=ctx_len, tc=tc)
    return pl.pallas_call(
        kern,
        out_shape=(jax.ShapeDtypeStruct((batch * seq, d), BF16),
                   jax.ShapeDtypeStruct((batch * ctx_len, d), BF16)),
        grid=(batch, nb),
        in_specs=[
            pl.BlockSpec((seq, w), lambda b, n: (b, n)),
            pl.BlockSpec((seq, w), lambda b, n: (b, nb + n)),
            pl.BlockSpec((ctx_len, w), lambda b, n: (b, n)),
            pl.BlockSpec((ctx_len, w), lambda b, n: (b, nb + n)),
            pl.BlockSpec((1, CONV_W, w), lambda b, n: (layer, 0, n)),
            pl.BlockSpec((1, 1, w), lambda b, n: (layer, 0, n)),
            pl.BlockSpec((1, w, 4 * w), lambda b, n: (n, 0, 0)),
            pl.BlockSpec((1, 1, 4 * w), lambda b, n: (n, 0, 0)),
            pl.BlockSpec((1, 2, w), lambda b, n: (layer, 0, n)),
        ],
        out_specs=(pl.BlockSpec((seq, w), lambda b, n: (b, n)),
                   pl.BlockSpec((ctx_len, w), lambda b, n: (b, n))),
        scratch_shapes=[
            pltpu.VMEM((n_slabs, seq + halo_rows, LANES), F32),
            pltpu.VMEM((n_slabs, ctx_len + halo_rows, LANES), F32),
            pltpu.VMEM((4, n_slabs, seq, LANES), F32),
            pltpu.VMEM((4, n_slabs, ctx_len, LANES), F32),
            pltpu.VMEM((2, N_SEGMENTS, w), F32),
            pltpu.VMEM((2, N_SEGMENTS, w), F32),
        ],
        compiler_params=pltpu.CompilerParams(
            dimension_semantics=("parallel", "parallel"), vmem_limit_bytes=_vmem_limit(block_bytes)),
        name="rglru",
    )(ug_l, ug_l, ug_c, ug_c, conv_w, conv_b, gw, gb, lam)


NA_QROWS = 4
NA_WIN_ROWS = 12
assert NA_WIN_ROWS >= NA_QROWS + NA_KH - 1 and (NA_WIN_ROWS * GRID_W) % MXU_DIM == 0
NA_BLOCKS_PER_STEP = 2
NA_TILE_ROWS = LANES // GRID_W
assert NA_TILE_ROWS == 2 and NA_WIN_ROWS % NA_TILE_ROWS == 0
N_BIAS_ROWS = 2 * NA_KH - 1
BIAS_BLOCKS = N_BIAS_ROWS + 3


def _nt_dot(a, b):
    return lax.dot_general(a, b, (((1,), (1,)), ((), ())), preferred_element_type=F32)


def _softmax_pv(scores, values):
    m = scores[0].max(axis=-1, keepdims=True)
    for s in scores[1:]:
        m = jnp.maximum(m, s.max(axis=-1, keepdims=True))
    ps = [jnp.exp(s - m) for s in scores]
    denom = ps[0].sum(axis=-1, keepdims=True)
    for p in ps[1:]:
        denom = denom + p.sum(axis=-1, keepdims=True)
    o = jnp.dot(ps[0].astype(BF16), values[0], preferred_element_type=F32)
    for p, v in zip(ps[1:], values[1:]):
        o = o + jnp.dot(p.astype(BF16), v, preferred_element_type=F32)
    return o / denom


def _tile_plan(rows):
    n_blocks = rows // NA_QROWS
    plans = {}
    for blk in range(n_blocks):
        kind = "first" if blk == 0 else ("last" if blk == n_blocks - 1 else "inner")
        win_start = min(max(blk * NA_QROWS - NA_KH // 2, 0), rows - NA_WIN_ROWS)
        plan = []
        for qi in range(NA_QROWS):
            r = blk * NA_QROWS + qi
            row_start = min(max(r - NA_KH // 2, 0), rows - NA_KH)
            assert win_start <= row_start and row_start + NA_KH <= win_start + NA_WIN_ROWS
            tiles = []
            for t in range(NA_WIN_ROWS // NA_TILE_ROWS):
                key_rows = [win_start + t * NA_TILE_ROWS + e for e in range(NA_TILE_ROWS)]
                inside = [row_start <= kr < row_start + NA_KH for kr in key_rows]
                if not any(inside):
                    tiles.append(None)
                    continue
                e0 = inside.index(True)
                block = 1 + (key_rows[e0] - r + NA_KH - 1) - e0
                assert 0 <= block and block + NA_TILE_ROWS <= BIAS_BLOCKS - 1
                keep = "both" if all(inside) else ("left" if inside[0] else "right")
                tiles.append((block % 2, (block - block % 2) * GRID_W, keep))
            plan.append(tuple(tiles))
        assert plans.setdefault(kind, tuple(plan)) == tuple(plan)
    return plans


def _attend_block(q, kw, vw, kc, vc, bias_ref, plan):
    s_lat = _nt_dot(q, kw)
    s_ctx = _nt_dot(q, kc)
    lane = lax.broadcasted_iota(jnp.int32, (GRID_W, LANES), 1)
    p_lat_rows, p_ctx_rows, denoms = [], [], []
    for qi, tiles in enumerate(plan):
        rs = slice(qi * GRID_W, (qi + 1) * GRID_W)
        scored = []
        for t, spec in enumerate(tiles):
            if spec is None:
                scored.append(None)
                continue
            copy, off, keep = spec
            s = s_lat[rs, t * LANES:(t + 1) * LANES] + bias_ref[0, copy, :, off:off + LANES]
            if keep == "left":
                s = jnp.where(lane < GRID_W, s, MASK_VALUE)
            elif keep == "right":
                s = jnp.where(lane >= GRID_W, s, MASK_VALUE)
            scored.append(s)
        ctx_tiles = [s_ctx[rs, t * LANES:(t + 1) * LANES] for t in range(s_ctx.shape[1] // LANES)]
        live = [s for s in scored if s is not None] + ctx_tiles
        m = functools.reduce(jnp.maximum, live).max(axis=-1, keepdims=True)
        p_lat = [None if s is None else jnp.exp(s - m) for s in scored]
        p_ctx = [jnp.exp(s - m) for s in ctx_tiles]
        total = functools.reduce(jnp.add, [p for p in p_lat if p is not None] + p_ctx)
        denoms.append(total.sum(axis=-1, keepdims=True))
        zero = jnp.zeros((GRID_W, LANES), BF16)
        p_lat_rows.append(jnp.concatenate([zero if p is None else p.astype(BF16) for p in p_lat], axis=1))
        p_ctx_rows.append(jnp.concatenate([p.astype(BF16) for p in p_ctx], axis=1))
    o = jnp.dot(jnp.concatenate(p_lat_rows, axis=0), vw, preferred_element_type=F32)
    o = o + jnp.dot(jnp.concatenate(p_ctx_rows, axis=0), vc, preferred_element_type=F32)
    return o / jnp.concatenate(denoms, axis=0)


def _natten_kernel(*refs, rows, need_ctx):
    if need_ctx:
        q_ref, k_ref, v_ref, g_ref, kc_ref, vc_ref, bias_ref, qc_ref, gc_ref, y_ref, yc_ref = refs
    else:
        q_ref, k_ref, v_ref, g_ref, kc_ref, vc_ref, bias_ref, y_ref = refs
    kc = kc_ref[...]
    vc = vc_ref[...]
    n_blocks = rows // NA_QROWS
    nq = NA_QROWS * GRID_W
    nk = NA_WIN_ROWS * GRID_W
    plans = _tile_plan(rows)

    def attend(blk, kind):
        if isinstance(blk, int):
            q0 = blk * nq
            k0 = min(max(blk * NA_QROWS - NA_KH // 2, 0), rows - NA_WIN_ROWS) * GRID_W
        else:
            q0 = pl.multiple_of(blk * nq, nq)
            k0 = pl.multiple_of(jnp.clip(blk * NA_QROWS - NA_KH // 2, 0, rows - NA_WIN_ROWS) * GRID_W, GRID_W)
        o = _attend_block(q_ref[pl.ds(q0, nq), :], k_ref[pl.ds(k0, nk), :], v_ref[pl.ds(k0, nk), :],
                          kc, vc, bias_ref, plans[kind])
        y_ref[pl.ds(q0, nq), :] = (o * _silu(g_ref[pl.ds(q0, nq), :])).astype(y_ref.dtype)

    def inner(it, carry):
        for sub in range(NA_BLOCKS_PER_STEP):
            attend(1 + it * NA_BLOCKS_PER_STEP + sub, "inner")
        return carry

    attend(0, "first")
    lax.fori_loop(0, (n_blocks - 2) // NA_BLOCKS_PER_STEP, inner, 0)
    attend(n_blocks - 1, "last")

    if need_ctx:
        oc = _softmax_pv([_nt_dot(qc_ref[...], kc)], [vc])
        yc_ref[...] = (oc * _silu(gc_ref[...])).astype(yc_ref.dtype)


def _bias_table(rpb):
    n_heads = rpb.shape[0]
    qc = np.arange(GRID_W)[:, None]
    kcol = np.arange(GRID_W)[None, :]
    col_start = np.clip(qc - NA_KW // 2, 0, GRID_W - NA_KW)
    inside = (kcol >= col_start) & (kcol < col_start + NA_KW)
    col_idx = np.clip(kcol - qc, -(NA_KW - 1), NA_KW - 1) + (NA_KW - 1)
    onehot = (col_idx[:, :, None] == np.arange(2 * NA_KW - 1)) & inside[:, :, None]
    tab = jnp.einsum('ckm,hjm->hcjk', jnp.asarray(onehot, F32), rpb, precision=lax.Precision.HIGHEST)
    tab = tab + jnp.asarray(np.where(inside, 0.0, MASK_VALUE), F32)[None, :, None, :]
    tab = tab.reshape(n_heads, GRID_W, N_BIAS_ROWS * GRID_W)

    def padded(left_blocks):
        right_blocks = BIAS_BLOCKS - N_BIAS_ROWS - left_blocks
        return jnp.pad(tab, ((0, 0), (0, 0), (left_blocks * GRID_W, right_blocks * GRID_W)),
                       constant_values=MASK_VALUE)

    return jnp.stack([padded(1), padded(0)], axis=1)


def _natten(qkv_l, g_l, kv_c, kv_c_head0, bias, batch, seq, ctx_len, d, g_c=None):
    n_heads = d // HEAD_DIM
    rows = seq // GRID_W
    need_ctx = g_c is not None
    hd = HEAD_DIM
    in_specs = [
        pl.BlockSpec((seq, hd), lambda b, h: (b, h)),
        pl.BlockSpec((seq, hd), lambda b, h: (b, n_heads + h)),
        pl.BlockSpec((seq, hd), lambda b, h: (b, 2 * n_heads + h)),
        pl.BlockSpec((seq, hd), lambda b, h: (b, h)),
        pl.BlockSpec((ctx_len, hd), lambda b, h: (b, kv_c_head0 + h)),
        pl.BlockSpec((ctx_len, hd), lambda b, h: (b, kv_c_head0 + n_heads + h)),
        pl.BlockSpec((1,) + bias.shape[1:], lambda b, h: (h, 0, 0, 0)),
    ]
    args = [qkv_l, qkv_l, qkv_l, g_l, kv_c, kv_c, bias]
    out_shape = [jax.ShapeDtypeStruct((batch * seq, d), BF16)]
    out_specs = [pl.BlockSpec((seq, hd), lambda b, h: (b, h))]
    if need_ctx:
        in_specs += [pl.BlockSpec((ctx_len, hd), lambda b, h: (b, h)),
                     pl.BlockSpec((ctx_len, hd), lambda b, h: (b, h))]
        args += [kv_c, g_c]
        out_shape.append(jax.ShapeDtypeStruct((batch * ctx_len, d), BF16))
        out_specs.append(pl.BlockSpec((ctx_len, hd), lambda b, h: (b, h)))
    nq, nk = NA_QROWS * GRID_W, NA_WIN_ROWS * GRID_W
    block_bytes = (2 * (4 * seq * hd * 2 + seq * hd * 4 + 4 * ctx_len * hd * 4)
                   + 2 * bias[0].size * 4 + 4 * NA_BLOCKS_PER_STEP * nq * (nk + ctx_len) * 4)
    kern = functools.partial(_natten_kernel, rows=rows, need_ctx=need_ctx)
    return pl.pallas_call(
        kern,
        out_shape=tuple(out_shape),
        grid=(batch, n_heads),
        in_specs=in_specs,
        out_specs=tuple(out_specs),
        compiler_params=pltpu.CompilerParams(
            dimension_semantics=("parallel", "parallel"), vmem_limit_bytes=_vmem_limit(block_bytes)),
        name="natten",
    )(*args)


def kernel(x, c, ctx, c_ctx, ada_w, ada_b, norm_g, lru_in_w, lru_conv_w, lru_conv_b, lru_gate_w,
           lru_gate_b, lru_lambda, lru_out_w, na_in_w, na_qk_norm, na_rpb, na_out_w):
    batch, seq, d = x.shape
    ctx_len = ctx.shape[1]
    depth = ada_w.shape[0]
    n_heads = d // HEAD_DIM
    rows = seq // GRID_W
    assert batch + 1 <= MOD_ROWS and seq % GRID_W == 0
    assert rows % NA_QROWS == 0 and rows >= NA_WIN_ROWS + NA_QROWS
    assert (rows // NA_QROWS - 2) % NA_BLOCKS_PER_STEP == 0 and ctx_len % LANES == 0

    xl = x.reshape(batch * seq, d)
    xc = ctx.reshape(batch * ctx_len, d)
    cvec = jnp.zeros((MOD_ROWS, d), F32).at[:batch].set(c).at[batch].set(c_ctx)
    mod = _adaln(cvec, ada_w, ada_b)
    lat = dict(rows_per_group=seq, group0=0)
    con = dict(rows_per_group=batch * ctx_len, group0=batch)
    conv_b = lru_conv_b.reshape(lru_conv_b.shape[0], 1, d)

    for i in range(depth):
        need_ctx = i < depth - 1
        j = i // 2
        hl = _norm_mod(xl, norm_g, mod, i, **lat)
        hc = _norm_mod(xc, norm_g, mod, i, **con)
        if i % 2 == 0:
            assert need_ctx
            ug_l = _proj(hl, lru_in_w, j, 0, 2 * d, F32)
            ug_c = _proj(hc, lru_in_w, j, 0, 2 * d, F32)
            y_l, y_c = _rglru(ug_l, ug_c, lru_conv_w, conv_b, lru_gate_w, lru_gate_b, lru_lambda,
                              j, batch, seq, ctx_len)
            w_out = lru_out_w
        else:
            norm_w = jnp.concatenate([jnp.tile(na_qk_norm[j, 0] * ATTN_SCALE, n_heads),
                                      jnp.tile(na_qk_norm[j, 1], n_heads),
                                      jnp.ones((d,), F32)]).reshape(1, 3 * d)
            bias = _bias_table(na_rpb[j])
            qkv = dict(mode="headnorm", norm_w=norm_w, n_norm_cols=2 * d)
            qkv_l = _proj(hl, na_in_w, j, 0, 3 * d, BF16, **qkv)
            g_l = _proj(hl, na_in_w, j, 3 * d, d, F32)
            if need_ctx:
                qkv_c = _proj(hc, na_in_w, j, 0, 3 * d, BF16, **qkv)
                g_c = _proj(hc, na_in_w, j, 3 * d, d, F32)
                y_l, y_c = _natten(qkv_l, g_l, qkv_c, n_heads, bias, batch, seq, ctx_len, d, g_c)
            else:
                kv_c = _proj(hc, na_in_w, j, d, 2 * d, BF16, **qkv)
                (y_l,) = _natten(qkv_l, g_l, kv_c, 0, bias, batch, seq, ctx_len, d)
                y_c = None
            w_out = na_out_w
        res = dict(mode="resid", mod=mod, mod_layer=i, gate_col0=2 * d)
        xl = _proj(y_l, w_out, j, 0, d, F32, x2=xl, **res, **lat)
        if need_ctx:
            xc = _proj(y_c, w_out, j, 0, d, F32, x2=xc, **res, **con)
    return xl.reshape(batch, seq, d)
```

```python
import functools
import math

import numpy as np
import jax
import jax.numpy as jnp
from jax import lax
from jax.experimental import pallas as pl
from jax.experimental.pallas import tpu as pltpu

GRID_W = 64
NA_KH = 8
NA_KW = 16
HEAD_DIM = 128
LRU_BLOCK = 256
CONV_W = 4
LRU_C = 8.0
EPS = 1e-6
ATTN_SCALE = HEAD_DIM ** -0.5
MASK_VALUE = -1e30

V7X_VMEM_BYTES = 64 * 1024 * 1024
SUBLANES = 8
LANES = 128
MXU_DIM = 256
MOD_ROWS = 8

F32 = jnp.float32
BF16 = jnp.bfloat16


def _vmem_limit(block_bytes):
    return int(min(block_bytes + (16 << 20), V7X_VMEM_BYTES - (4 << 20)))


def _silu(v):
    half = 0.5 * v
    return half * jnp.tanh(half) + half


def _adaln_kernel(c_ref, w_ref, b_ref, o_ref):
    cond = _silu(c_ref[...]).astype(BF16)
    w = w_ref[0].astype(BF16)
    o_ref[0] = jnp.dot(cond, w, preferred_element_type=F32) + b_ref[0]


def _adaln(cvec, ada_w, ada_b):
    n_layers, d, n3 = ada_w.shape
    tn = min(512, n3)
    block_bytes = 2 * d * tn * 4 + d * tn * 2 + 4 * MOD_ROWS * (d + 2 * tn) * 4
    return pl.pallas_call(
        _adaln_kernel,
        out_shape=jax.ShapeDtypeStruct((n_layers, MOD_ROWS, n3), F32),
        grid=(n_layers, n3 // tn),
        in_specs=[
            pl.BlockSpec((MOD_ROWS, d), lambda l, j: (0, 0)),
            pl.BlockSpec((1, d, tn), lambda l, j: (l, 0, j)),
            pl.BlockSpec((1, 1, tn), lambda l, j: (l, 0, j)),
        ],
        out_specs=pl.BlockSpec((1, MOD_ROWS, tn), lambda l, j: (l, 0, j)),
        compiler_params=pltpu.CompilerParams(
            dimension_semantics=("parallel", "parallel"), vmem_limit_bytes=_vmem_limit(block_bytes)),
        name="adaln",
    )(cvec, ada_w, ada_b.reshape(n_layers, 1, n3))


def _norm_kernel(x_ref, g_ref, mod_ref, o_ref, *, tiles_per_group, group0, d):
    grp = group0 + pl.program_id(0) // tiles_per_group
    x = x_ref[...]
    ms = jnp.mean(x * x, axis=-1, keepdims=True)
    y = x * lax.rsqrt(ms + EPS) * g_ref[0]
    shift = mod_ref[0, pl.ds(grp, 1), 0:d]
    scale = mod_ref[0, pl.ds(grp, 1), d:2 * d]
    o_ref[...] = (y * (1.0 + scale) + shift).astype(BF16)


def _norm_mod(x2, norm_g, mod, layer, rows_per_group, group0):
    m, d = x2.shape
    tm = min(256, rows_per_group)
    block_bytes = 2 * tm * d * 4 + 2 * tm * d * 2 + 2 * MOD_ROWS * 3 * d * 4 + 3 * tm * d * 4
    kern = functools.partial(_norm_kernel, tiles_per_group=rows_per_group // tm, group0=group0, d=d)
    return pl.pallas_call(
        kern,
        out_shape=jax.ShapeDtypeStruct((m, d), BF16),
        grid=(m // tm,),
        in_specs=[
            pl.BlockSpec((tm, d), lambda i: (i, 0)),
            pl.BlockSpec((1, 1, d), lambda i: (layer, 0, 0)),
            pl.BlockSpec((1, MOD_ROWS, 3 * d), lambda i: (layer, 0, 0)),
        ],
        out_specs=pl.BlockSpec((tm, d), lambda i: (i, 0)),
        compiler_params=pltpu.CompilerParams(
            dimension_semantics=("parallel",), vmem_limit_bytes=_vmem_limit(block_bytes)),
        name="norm_mod",
    )(x2, norm_g.reshape(norm_g.shape[0], 1, d), mod)


PROJ_ROW_BLOCK = 256


def _proj_tile(a_ref, w, o_ref, extra, row_tile, col_tile, *, mode, tiles_per_group, group0, n_norm_tiles):
    tm = a_ref.shape[0]
    row_block = min(PROJ_ROW_BLOCK, tm)

    def blocks(epilogue):
        for r0 in range(0, tm, row_block):
            rows = slice(r0, r0 + row_block)
            epilogue(rows, jnp.dot(a_ref[rows, :], w, preferred_element_type=F32))

    def plain(rows, acc):
        o_ref[rows, :] = acc.astype(o_ref.dtype)

    if mode == "resid":
        x_ref, gate_ref = extra
        gate = gate_ref[0, pl.ds(group0 + row_tile // tiles_per_group, 1), :]

        def resid(rows, acc):
            o_ref[rows, :] = x_ref[rows, :] + gate * acc

        blocks(resid)
    elif mode == "headnorm":
        (nw_ref,) = extra

        def headnorm(rows, acc):
            for hh in range(acc.shape[1] // HEAD_DIM):
                cols = slice(hh * HEAD_DIM, (hh + 1) * HEAD_DIM)
                blk = acc[:, cols]
                ms = jnp.mean(blk * blk, axis=-1, keepdims=True)
                o_ref[rows, cols] = (blk * lax.rsqrt(ms + EPS) * nw_ref[:, cols]).astype(o_ref.dtype)

        pl.when(col_tile < n_norm_tiles)(lambda: blocks(headnorm))
        pl.when(col_tile >= n_norm_tiles)(lambda: blocks(plain))
    else:
        blocks(plain)


def _proj_kernel(a_ref, w_ref, *rest, j0, **epilogue):
    *extra, o_ref, wb = rest
    j = pl.program_id(0)
    i = pl.program_id(1)

    @pl.when(i == 0)
    def _():
        wb[...] = w_ref[0].astype(BF16)

    _proj_tile(a_ref, wb[...], o_ref, extra, i, j + j0, **epilogue)


def _proj_stream_kernel(a_ref, w_ref, *rest, j0, chunk_rows, **epilogue):
    *extra, o_ref, wb = rest
    jj = pl.program_id(0)
    i = pl.program_id(1)

    def round_chunk():
        r0 = pl.multiple_of(i * chunk_rows, chunk_rows)
        wb[jj % 2, pl.ds(r0, chunk_rows), :] = w_ref[0].astype(BF16)

    @pl.when(jj == 0)
    def _():
        round_chunk()

    @pl.when(jj > 0)
    def _():
        round_chunk()
        _proj_tile(a_ref, wb[(jj - 1) % 2], o_ref, extra, i, jj - 1 + j0, **epilogue)


def _proj(a, w_stack, layer, col0, n_out, out_dtype, *, mode="plain", norm_w=None, n_norm_cols=0,
          x2=None, mod=None, mod_layer=0, gate_col0=0, rows_per_group=None, group0=0):
    m, k = a.shape
    rows_per_group = rows_per_group or m
    tm = min(1024, rows_per_group)
    n_row_tiles = m // tm
    stream = n_row_tiles > 1
    tn = 512 if mode == "resid" or not stream else 1024
    while any(extent % tn for extent in (n_out, col0, n_norm_cols, gate_col0)):
        tn //= 2
    assert tn >= LANES
    n_col_tiles = n_out // tn
    j0 = col0 // tn
    out_bytes = jnp.dtype(out_dtype).itemsize
    block_bytes = 2 * tm * k * 2 + 2 * tm * tn * out_bytes + tm * tn * 4
    epilogue = dict(mode=mode, tiles_per_group=rows_per_group // tm, group0=group0,
                    n_norm_tiles=n_norm_cols // tn)
    if stream:
        assert k % n_row_tiles == 0
        chunk_rows = k // n_row_tiles
        block_bytes += 2 * chunk_rows * tn * 4 + 2 * k * tn * 2
        grid = (n_col_tiles + 1, n_row_tiles)

        def row_of(jj, i):
            return jnp.where(jj == 0, 0, i)

        def col_of(jj):
            return jnp.maximum(jj - 1, 0)

        in_specs = [
            pl.BlockSpec((tm, k), lambda jj, i: (row_of(jj, i), 0)),
            pl.BlockSpec((1, chunk_rows, tn), lambda jj, i: (
                layer, jnp.where(jj == n_col_tiles, 0, i), j0 + jnp.minimum(jj, n_col_tiles - 1))),
        ]
        scratch = pltpu.VMEM((2, k, tn), BF16)
        kern = functools.partial(_proj_stream_kernel, j0=j0, chunk_rows=chunk_rows, **epilogue)
    else:
        block_bytes += 2 * k * tn * 4 + k * tn * 2
        grid = (n_col_tiles, n_row_tiles)

        def row_of(jj, i):
            return i

        def col_of(jj):
            return jj

        in_specs = [
            pl.BlockSpec((tm, k), lambda jj, i: (i, 0)),
            pl.BlockSpec((1, k, tn), lambda jj, i: (layer, 0, j0 + jj)),
        ]
        scratch = pltpu.VMEM((k, tn), BF16)
        kern = functools.partial(_proj_kernel, j0=j0, **epilogue)
    args = [a, w_stack]
    if mode == "headnorm":
        in_specs.append(pl.BlockSpec((1, tn), lambda jj, i: (0, j0 + col_of(jj))))
        args.append(norm_w)
    elif mode == "resid":
        g0 = gate_col0 // tn
        in_specs += [pl.BlockSpec((tm, tn), lambda jj, i: (row_of(jj, i), col_of(jj))),
                     pl.BlockSpec((1, MOD_ROWS, tn), lambda jj, i: (mod_layer, 0, g0 + col_of(jj)))]
        args += [x2, mod]
        block_bytes += 2 * tm * tn * 4
    return pl.pallas_call(
        kern,
        out_shape=jax.ShapeDtypeStruct((m, n_out), out_dtype),
        grid=grid,
        in_specs=in_specs,
        out_specs=pl.BlockSpec((tm, tn), lambda jj, i: (row_of(jj, i), col_of(jj))),
        scratch_shapes=[scratch],
        compiler_params=pltpu.CompilerParams(
            dimension_semantics=("arbitrary", "arbitrary"), vmem_limit_bytes=_vmem_limit(block_bytes)),
        name="proj_" + mode,
    )(*args)


CONV_LEFT = CONV_W // 2
N_SEGMENTS = SUBLANES
SCAN_BLOCK = 8


def _scan_rows(c, tc, seg_len):
    if seg_len >= tc:
        chunks_per_seg = seg_len // tc
        s = lax.div(c, chunks_per_seg)
        k0 = lax.rem(c, chunks_per_seg) * tc
        return [(0, tc, s + N_SEGMENTS * k0, s)]
    segs_per_chunk = tc // seg_len
    return [(e * seg_len, seg_len, c * segs_per_chunk + e, c * segs_per_chunk + e) for e in range(segs_per_chunk)]


def _lru_to_scan_order(src_ref, us_ref, length, tc):
    n_slabs = us_ref.shape[0]
    seg_len = length // N_SEGMENTS
    lead = CONV_LEFT * N_SEGMENTS

    def body(c, carry):
        base = pl.multiple_of(c * tc, tc)
        for row0, n_rows, scan_row0, _ in _scan_rows(c, tc, seg_len):
            dst = pl.ds(lead + scan_row0, n_rows, stride=N_SEGMENTS)
            for slab in range(n_slabs):
                us_ref[slab, dst, :] = src_ref[pl.ds(base + row0, n_rows), slab * LANES:(slab + 1) * LANES]
        return carry

    lax.fori_loop(0, length // tc, body, 0)

    seg = lax.broadcasted_iota(jnp.int32, (N_SEGMENTS, LANES), 0)
    for slab in range(n_slabs):
        def step(k):
            return us_ref[slab, pl.ds(lead + k * N_SEGMENTS, N_SEGMENTS), :]

        for back in range(1, CONV_LEFT + 1):
            halo = jnp.where(seg >= 1, pltpu.roll(step(seg_len - back), 1, 0), 0.0)
            us_ref[slab, pl.ds(lead - back * N_SEGMENTS, N_SEGMENTS), :] = halo
        for ahead in range(CONV_W - 1 - CONV_LEFT):
            halo = jnp.where(seg < N_SEGMENTS - 1, pltpu.roll(step(ahead), N_SEGMENTS - 1, 0), 0.0)
            us_ref[slab, pl.ds(lead + (seg_len + ahead) * N_SEGMENTS, N_SEGMENTS), :] = halo


def _lru_coefficients(us_ref, co_ref, cw_ref, cb_ref, gw_ref, gb_ref, cvec_log2, length, tc):
    n_slabs = us_ref.shape[0]
    width = n_slabs * LANES

    def body(c, carry):
        base = pl.multiple_of(c * tc, tc)
        win = jnp.concatenate([us_ref[slab, pl.ds(base, tc + (CONV_W - 1) * N_SEGMENTS), :]
                               for slab in range(n_slabs)], axis=1)
        u = cw_ref[0:1, :] * win[0:tc]
        for kk in range(1, CONV_W):
            u = u + cw_ref[kk:kk + 1, :] * win[kk * N_SEGMENTS:kk * N_SEGMENTS + tc]
        u = u + cb_ref[...]
        zh = jnp.dot(u.astype(BF16), gw_ref[0], preferred_element_type=F32) + gb_ref[0]
        u_half = 0.5 * u
        c_half = 0.5 * cvec_log2
        for d in range(2):
            th_r = jnp.tanh(zh[:, 2 * d * width:(2 * d + 1) * width])
            th_i = jnp.tanh(zh[:, (2 * d + 1) * width:(2 * d + 2) * width])
            a = jnp.exp2(th_r * c_half[d:d + 1] + c_half[d:d + 1])
            gated_u = th_i * u_half + u_half
            t = 1.0 - a * a
            b = jnp.where(t > 0.0, t * lax.rsqrt(t), 0.0) * gated_u
            for slab in range(n_slabs):
                lanes = slice(slab * LANES, (slab + 1) * LANES)
                co_ref[2 * d, slab, pl.ds(base, tc), :] = a[:, lanes]
                co_ref[2 * d + 1, slab, pl.ds(base, tc), :] = b[:, lanes]
        return carry

    lax.fori_loop(0, length // tc, body, 0)


def _lru_local_scan(co_ref, length):
    n_slabs = co_ref.shape[1]
    n_blocks = length // N_SEGMENTS // SCAN_BLOCK
    block_rows = SCAN_BLOCK * N_SEGMENTS
    zero = jnp.zeros((N_SEGMENTS, LANES), F32)
    one = jnp.ones((N_SEGMENTS, LANES), F32)

    def body(i, state):
        rows = (pl.ds(pl.multiple_of(i * block_rows, block_rows), block_rows),
                pl.ds(pl.multiple_of((n_blocks - 1 - i) * block_rows, block_rows), block_rows))
        new_state = []
        for d in range(2):
            for slab in range(n_slabs):
                h, p = state[d * n_slabs + slab]
                a_blk = co_ref[2 * d, slab, rows[d], :]
                b_blk = co_ref[2 * d + 1, slab, rows[d], :]
                hs, ps = [None] * SCAN_BLOCK, [None] * SCAN_BLOCK
                for j in (range(SCAN_BLOCK) if d == 0 else range(SCAN_BLOCK - 1, -1, -1)):
                    step = slice(j * N_SEGMENTS, (j + 1) * N_SEGMENTS)
                    h = a_blk[step] * h + b_blk[step]
                    p = a_blk[step] * p
                    hs[j], ps[j] = h, p
                co_ref[2 * d, slab, rows[d], :] = jnp.concatenate(ps, axis=0)
                co_ref[2 * d + 1, slab, rows[d], :] = jnp.concatenate(hs, axis=0)
                new_state.append((h, p))
        return tuple(new_state)

    lax.fori_loop(0, n_blocks, body, tuple((zero, one) for _ in range(2 * n_slabs)))


def _lru_carries(co_ref, carry_ref, length, h0_fwd, h0_bwd):
    n_slabs = co_ref.shape[1]
    seg_len = length // N_SEGMENTS
    ends = (pl.ds((seg_len - 1) * N_SEGMENTS, N_SEGMENTS), pl.ds(0, N_SEGMENTS))
    finals = []
    for d, h0 in enumerate((h0_fwd, h0_bwd)):
        p_end = jnp.concatenate([co_ref[2 * d, slab, ends[d], :] for slab in range(n_slabs)], axis=1)
        h_end = jnp.concatenate([co_ref[2 * d + 1, slab, ends[d], :] for slab in range(n_slabs)], axis=1)
        carry = h0
        for s in (range(N_SEGMENTS) if d == 0 else range(N_SEGMENTS - 1, -1, -1)):
            carry_ref[d, pl.ds(s, 1), :] = carry
            carry = h_end[s:s + 1] + p_end[s:s + 1] * carry
        finals.append(carry)
    return finals


def _lru_output(co_ref, carry_ref, gate_ref, out_ref, length, tc):
    n_slabs = co_ref.shape[1]
    seg_len = length // N_SEGMENTS
    steps = tc // N_SEGMENTS

    def combine(c, carry):
        rows = pl.ds(pl.multiple_of(c * tc, tc), tc)
        for slab in range(n_slabs):
            lanes = slice(slab * LANES, (slab + 1) * LANES)
            hsum = None
            for d in range(2):
                cin = carry_ref[d, :, lanes][None]
                p = co_ref[2 * d, slab, rows, :].reshape(steps, N_SEGMENTS, LANES)
                h = co_ref[2 * d + 1, slab, rows, :].reshape(steps, N_SEGMENTS, LANES) + p * cin
                hsum = h if hsum is None else hsum + h
            co_ref[1, slab, rows, :] = hsum.reshape(tc, LANES)
        return carry

    lax.fori_loop(0, length // tc, combine, 0)

    def emit(c, carry):
        base = pl.multiple_of(c * tc, tc)
        for row0, n_rows, scan_row0, _ in _scan_rows(c, tc, seg_len):
            src = pl.ds(scan_row0, n_rows, stride=N_SEGMENTS)
            hsum = jnp.concatenate([co_ref[1, slab, src, :] for slab in range(n_slabs)], axis=1)
            rows = pl.ds(base + row0, n_rows)
            out_ref[rows, :] = (hsum * _silu(gate_ref[rows, :])).astype(out_ref.dtype)
        return carry

    lax.fori_loop(0, length // tc, emit, 0)


def _rglru_kernel(u_ref, g_ref, uc_ref, gc_ref, cw_ref, cb_ref, gw_ref, gb_ref, lam_ref,
                  y_ref, yc_ref, us_l, us_c, co_l, co_c, carry_l, carry_c, *, seq, ctx_len, tc):
    width = u_ref.shape[1]
    lam = lam_ref[0]
    softplus_neg = jnp.maximum(-lam, 0.0) + jnp.log1p(jnp.exp(-jnp.abs(lam)))
    cvec_log2 = (-LRU_C * math.log2(math.e)) * softplus_neg
    params = (cw_ref.at[0], cb_ref.at[0], gw_ref, gb_ref, cvec_log2)

    zeros = jnp.zeros((1, width), F32)
    _lru_to_scan_order(uc_ref, us_c, ctx_len, tc)
    _lru_coefficients(us_c, co_c, *params, ctx_len, tc)
    _lru_local_scan(co_c, ctx_len)
    h_fwd, h_bwd = _lru_carries(co_c, carry_c, ctx_len, zeros, zeros)
    _lru_output(co_c, carry_c, gc_ref, yc_ref, ctx_len, tc)

    _lru_to_scan_order(u_ref, us_l, seq, tc)
    _lru_coefficients(us_l, co_l, *params, seq, tc)
    _lru_local_scan(co_l, seq)
    _lru_carries(co_l, carry_l, seq, h_fwd, h_bwd)
    _lru_output(co_l, carry_l, g_ref, y_ref, seq, tc)


def _rglru(ug_l, ug_c, conv_w, conv_b, gate_w, gate_b, lam, layer, batch, seq, ctx_len):
    d = conv_w.shape[2]
    nb = d // LRU_BLOCK
    w = LRU_BLOCK
    tc = min(256, ctx_len, seq)
    for length in (seq, ctx_len):
        seg_len = length // N_SEGMENTS
        assert length % (N_SEGMENTS * SCAN_BLOCK) == 0 and length % tc == 0
        assert seg_len % tc == 0 or tc % seg_len == 0
    n_slabs = w // LANES
    halo_rows = (CONV_W - 1) * N_SEGMENTS
    gw = (0.5 * jnp.transpose(gate_w[layer], (2, 3, 0, 1, 4))).reshape(nb, w, 4 * w).astype(BF16)
    gb = 0.5 * jnp.transpose(gate_b[layer].reshape(2, 2, nb, w), (2, 0, 1, 3)).reshape(nb, 1, 4 * w)
    block_bytes = (2 * 2 * (seq + ctx_len) * w * 4 + 2 * (seq + ctx_len) * w * 2 + 2 * w * 4 * w * 2
                   + (5 * (seq + ctx_len) + 2 * halo_rows) * w * 4 + 12 * tc * w * 4)
    kern = functools.partial(_rglru_kernel, seq=seq, ctx_len=ctx_len, tc=tc)
    return pl.pallas_call(
        kern,
        out_shape=(jax.ShapeDtypeStruct((batch * seq, d), BF16),
                   jax.ShapeDtypeStruct((batch * ctx_len, d), BF16)),
        grid=(batch, nb),
        in_specs=[
            pl.BlockSpec((seq, w), lambda b, n: (b, n)),
            pl.BlockSpec((seq, w), lambda b, n: (b, nb + n)),
            pl.BlockSpec((ctx_len, w), lambda b, n: (b, n)),
            pl.BlockSpec((ctx_len, w), lambda b, n: (b, nb + n)),
            pl.BlockSpec((1, CONV_W, w), lambda b, n: (layer, 0, n)),
            pl.BlockSpec((1, 1, w), lambda b, n: (layer, 0, n)),
            pl.BlockSpec((1, w, 4 * w), lambda b, n: (n, 0, 0)),
            pl.BlockSpec((1, 1, 4 * w), lambda b, n: (n, 0, 0)),
            pl.BlockSpec((1, 2, w), lambda b, n: (layer, 0, n)),
        ],
        out_specs=(pl.BlockSpec((seq, w), lambda b, n: (b, n)),
                   pl.BlockSpec((ctx_len, w), lambda b, n: (b, n))),
        scratch_shapes=[
            pltpu.VMEM((n_slabs, seq + halo_rows, LANES), F32),
            pltpu.VMEM((n_slabs, ctx_len + halo_rows, LANES), F32),
            pltpu.VMEM((4, n_slabs, seq, LANES), F32),
            pltpu.VMEM((4, n_slabs, ctx_len, LANES), F32),
            pltpu.VMEM((2, N_SEGMENTS, w), F32),
            pltpu.VMEM((2, N_SEGMENTS, w), F32),
        ],
        compiler_params=pltpu.CompilerParams(
            dimension_semantics=("parallel", "parallel"), vmem_limit_bytes=_vmem_limit(block_bytes)),
        name="rglru",
    )(ug_l, ug_l, ug_c, ug_c, conv_w, conv_b, gw, gb, lam)


NA_QROWS = 4
NA_WIN_ROWS = 12
assert NA_WIN_ROWS >= NA_QROWS + NA_KH - 1 and (NA_WIN_ROWS * GRID_W) % MXU_DIM == 0
NA_BLOCKS_PER_STEP = 7
NA_TILE_ROWS = LANES // GRID_W
assert NA_TILE_ROWS == 2 and NA_WIN_ROWS % NA_TILE_ROWS == 0
N_BIAS_ROWS = 2 * NA_KH - 1
BIAS_BLOCKS = N_BIAS_ROWS + 3


def _nt_dot(a, b):
    return lax.dot_general(a, b, (((1,), (1,)), ((), ())), preferred_element_type=F32)


def _softmax_pv(scores, values):
    m = scores[0].max(axis=-1, keepdims=True)
    for s in scores[1:]:
        m = jnp.maximum(m, s.max(axis=-1, keepdims=True))
    ps = [jnp.exp(s - m) for s in scores]
    denom = ps[0].sum(axis=-1, keepdims=True)
    for p in ps[1:]:
        denom = denom + p.sum(axis=-1, keepdims=True)
    o = jnp.dot(ps[0].astype(BF16), values[0], preferred_element_type=F32)
    for p, v in zip(ps[1:], values[1:]):
        o = o + jnp.dot(p.astype(BF16), v, preferred_element_type=F32)
    return o / denom


def _tile_plan(rows):
    n_blocks = rows // NA_QROWS
    plans = {}
    for blk in range(n_blocks):
        kind = "first" if blk == 0 else ("last" if blk == n_blocks - 1 else "inner")
        win_start = min(max(blk * NA_QROWS - NA_KH // 2, 0), rows - NA_WIN_ROWS)
        plan = []
        for qi in range(NA_QROWS):
            r = blk * NA_QROWS + qi
            row_start = min(max(r - NA_KH // 2, 0), rows - NA_KH)
            assert win_start <= row_start and row_start + NA_KH <= win_start + NA_WIN_ROWS
            tiles = []
            for t in range(NA_WIN_ROWS // NA_TILE_ROWS):
                key_rows = [win_start + t * NA_TILE_ROWS + e for e in range(NA_TILE_ROWS)]
                inside = [row_start <= kr < row_start + NA_KH for kr in key_rows]
                if not any(inside):
                    tiles.append(None)
                    continue
                e0 = inside.index(True)
                block = 1 + (key_rows[e0] - r + NA_KH - 1) - e0
                assert 0 <= block and block + NA_TILE_ROWS <= BIAS_BLOCKS - 1
                keep = "both" if all(inside) else ("left" if inside[0] else "right")
                tiles.append((block % 2, (block - block % 2) * GRID_W, keep))
            plan.append(tuple(tiles))
        assert plans.setdefault(kind, tuple(plan)) == tuple(plan)
    return plans


def _attend_block(q, kw, vw, kc, vc, bias_ref, plan):
    s_lat = _nt_dot(q, kw)
    s_ctx = _nt_dot(q, kc)
    lane = lax.broadcasted_iota(jnp.int32, (GRID_W, LANES), 1)
    p_lat_rows, p_ctx_rows, denoms = [], [], []
    for qi, tiles in enumerate(plan):
        rs = slice(qi * GRID_W, (qi + 1) * GRID_W)
        scored = []
        for t, spec in enumerate(tiles):
            if spec is None:
                scored.append(None)
                continue
            copy, off, keep = spec
            s = s_lat[rs, t * LANES:(t + 1) * LANES] + bias_ref[0, copy, :, off:off + LANES]
            if keep == "left":
                s = jnp.where(lane < GRID_W, s, MASK_VALUE)
            elif keep == "right":
                s = jnp.where(lane >= GRID_W, s, MASK_VALUE)
            scored.append(s)
        ctx_tiles = [s_ctx[rs, t * LANES:(t + 1) * LANES] for t in range(s_ctx.shape[1] // LANES)]
        live = [s for s in scored if s is not None] + ctx_tiles
        m = functools.reduce(jnp.maximum, live).max(axis=-1, keepdims=True)
        p_lat = [None if s is None else jnp.exp(s - m) for s in scored]
        p_ctx = [jnp.exp(s - m) for s in ctx_tiles]
        total = functools.reduce(jnp.add, [p for p in p_lat if p is not None] + p_ctx)
        denoms.append(total.sum(axis=-1, keepdims=True))
        zero = jnp.zeros((GRID_W, LANES), BF16)
        p_lat_rows.append(jnp.concatenate([zero if p is None else p.astype(BF16) for p in p_lat], axis=1))
        p_ctx_rows.append(jnp.concatenate([p.astype(BF16) for p in p_ctx], axis=1))
    o = jnp.dot(jnp.concatenate(p_lat_rows, axis=0), vw, preferred_element_type=F32)
    o = o + jnp.dot(jnp.concatenate(p_ctx_rows, axis=0), vc, preferred_element_type=F32)
    return o / jnp.concatenate(denoms, axis=0)


def _natten_kernel(*refs, rows, need_ctx):
    if need_ctx:
        q_ref, k_ref, v_ref, g_ref, kc_ref, vc_ref, bias_ref, qc_ref, gc_ref, y_ref, yc_ref = refs
    else:
        q_ref, k_ref, v_ref, g_ref, kc_ref, vc_ref, bias_ref, y_ref = refs
    kc = kc_ref[...]
    vc = vc_ref[...]
    n_blocks = rows // NA_QROWS
    nq = NA_QROWS * GRID_W
    nk = NA_WIN_ROWS * GRID_W
    plans = _tile_plan(rows)

    def attend(blk, kind):
        if isinstance(blk, int):
            q0 = blk * nq
            k0 = min(max(blk * NA_QROWS - NA_KH // 2, 0), rows - NA_WIN_ROWS) * GRID_W
        else:
            q0 = pl.multiple_of(blk * nq, nq)
            k0 = pl.multiple_of(jnp.clip(blk * NA_QROWS - NA_KH // 2, 0, rows - NA_WIN_ROWS) * GRID_W, GRID_W)
        o = _attend_block(q_ref[pl.ds(q0, nq), :], k_ref[pl.ds(k0, nk), :], v_ref[pl.ds(k0, nk), :],
                          kc, vc, bias_ref, plans[kind])
        y_ref[pl.ds(q0, nq), :] = (o * _silu(g_ref[pl.ds(q0, nq), :])).astype(y_ref.dtype)

    def inner(it, carry):
        for sub in range(NA_BLOCKS_PER_STEP):
            attend(1 + it * NA_BLOCKS_PER_STEP + sub, "inner")
        return carry

    attend(0, "first")
    lax.fori_loop(0, (n_blocks - 2) // NA_BLOCKS_PER_STEP, inner, 0)
    attend(n_blocks - 1, "last")

    if need_ctx:
        oc = _softmax_pv([_nt_dot(qc_ref[...], kc)], [vc])
        yc_ref[...] = (oc * _silu(gc_ref[...])).astype(yc_ref.dtype)


def _bias_table(rpb):
    n_heads = rpb.shape[0]
    qc = np.arange(GRID_W)[:, None]
    kcol = np.arange(GRID_W)[None, :]
    col_start = np.clip(qc - NA_KW // 2, 0, GRID_W - NA_KW)
    inside = (kcol >= col_start) & (kcol < col_start + NA_KW)
    col_idx = np.clip(kcol - qc, -(NA_KW - 1), NA_KW - 1) + (NA_KW - 1)
    onehot = (col_idx[:, :, None] == np.arange(2 * NA_KW - 1)) & inside[:, :, None]
    tab = jnp.einsum('ckm,hjm->hcjk', jnp.asarray(onehot, F32), rpb, precision=lax.Precision.HIGHEST)
    tab = tab + jnp.asarray(np.where(inside, 0.0, MASK_VALUE), F32)[None, :, None, :]
    tab = tab.reshape(n_heads, GRID_W, N_BIAS_ROWS * GRID_W)

    def padded(left_blocks):
        right_blocks = BIAS_BLOCKS - N_BIAS_ROWS - left_blocks
        return jnp.pad(tab, ((0, 0), (0, 0), (left_blocks * GRID_W, right_blocks * GRID_W)),
                       constant_values=MASK_VALUE)

    return jnp.stack([padded(1), padded(0)], axis=1)


def _natten(qkv_l, g_l, kv_c, kv_c_head0, bias, batch, seq, ctx_len, d, g_c=None):
    n_heads = d // HEAD_DIM
    rows = seq // GRID_W
    need_ctx = g_c is not None
    hd = HEAD_DIM
    in_specs = [
        pl.BlockSpec((seq, hd), lambda b, h: (b, h)),
        pl.BlockSpec((seq, hd), lambda b, h: (b, n_heads + h)),
        pl.BlockSpec((seq, hd), lambda b, h: (b, 2 * n_heads + h)),
        pl.BlockSpec((seq, hd), lambda b, h: (b, h)),
        pl.BlockSpec((ctx_len, hd), lambda b, h: (b, kv_c_head0 + h)),
        pl.BlockSpec((ctx_len, hd), lambda b, h: (b, kv_c_head0 + n_heads + h)),
        pl.BlockSpec((1,) + bias.shape[1:], lambda b, h: (h, 0, 0, 0)),
    ]
    args = [qkv_l, qkv_l, qkv_l, g_l, kv_c, kv_c, bias]
    out_shape = [jax.ShapeDtypeStruct((batch * seq, d), BF16)]
    out_specs = [pl.BlockSpec((seq, hd), lambda b, h: (b, h))]
    if need_ctx:
        in_specs += [pl.BlockSpec((ctx_len, hd), lambda b, h: (b, h)),
                     pl.BlockSpec((ctx_len, hd), lambda b, h: (b, h))]
        args += [kv_c, g_c]
        out_shape.append(jax.ShapeDtypeStruct((batch * ctx_len, d), BF16))
        out_specs.append(pl.BlockSpec((ctx_len, hd), lambda b, h: (b, h)))
    nq, nk = NA_QROWS * GRID_W, NA_WIN_ROWS * GRID_W
    block_bytes = (2 * (4 * seq * hd * 2 + seq * hd * 4 + 4 * ctx_len * hd * 4)
                   + 2 * bias[0].size * 4 + 4 * NA_BLOCKS_PER_STEP * nq * (nk + ctx_len) * 4)
    kern = functools.partial(_natten_kernel, rows=rows, need_ctx=need_ctx)
    return pl.pallas_call(
        kern,
        out_shape=tuple(out_shape),
        grid=(batch, n_heads),
        in_specs=in_specs,
        out_specs=tuple(out_specs),
        compiler_params=pltpu.CompilerParams(
            dimension_semantics=("parallel", "parallel"), vmem_limit_bytes=_vmem_limit(block_bytes)),
        name="natten",
    )(*args)


def kernel(x, c, ctx, c_ctx, ada_w, ada_b, norm_g, lru_in_w, lru_conv_w, lru_conv_b, lru_gate_w,
           lru_gate_b, lru_lambda, lru_out_w, na_in_w, na_qk_norm, na_rpb, na_out_w):
    batch, seq, d = x.shape
    ctx_len = ctx.shape[1]
    depth = ada_w.shape[0]
    n_heads = d // HEAD_DIM
    rows = seq // GRID_W
    assert batch + 1 <= MOD_ROWS and seq % GRID_W == 0
    assert rows % NA_QROWS == 0 and rows >= NA_WIN_ROWS + NA_QROWS
    assert (rows // NA_QROWS - 2) % NA_BLOCKS_PER_STEP == 0 and ctx_len % LANES == 0

    xl = x.reshape(batch * seq, d)
    xc = ctx.reshape(batch * ctx_len, d)
    cvec = jnp.zeros((MOD_ROWS, d), F32).at[:batch].set(c).at[batch].set(c_ctx)
    mod = _adaln(cvec, ada_w, ada_b)
    lat = dict(rows_per_group=seq, group0=0)
    con = dict(rows_per_group=batch * ctx_len, group0=batch)
    conv_b = lru_conv_b.reshape(lru_conv_b.shape[0], 1, d)

    for i in range(depth):
        need_ctx = i < depth - 1
        j = i // 2
        hl = _norm_mod(xl, norm_g, mod, i, **lat)
        hc = _norm_mod(xc, norm_g, mod, i, **con)
        if i % 2 == 0:
            assert need_ctx
            ug_l = _proj(hl, lru_in_w, j, 0, 2 * d, F32)
            ug_c = _proj(hc, lru_in_w, j, 0, 2 * d, F32)
            y_l, y_c = _rglru(ug_l, ug_c, lru_conv_w, conv_b, lru_gate_w, lru_gate_b, lru_lambda,
                              j, batch, seq, ctx_len)
            w_out = lru_out_w
        else:
            norm_w = jnp.concatenate([jnp.tile(na_qk_norm[j, 0] * ATTN_SCALE, n_heads),
                                      jnp.tile(na_qk_norm[j, 1], n_heads),
                                      jnp.ones((d,), F32)]).reshape(1, 3 * d)
            bias = _bias_table(na_rpb[j])
            qkv = dict(mode="headnorm", norm_w=norm_w, n_norm_cols=2 * d)
            qkv_l = _proj(hl, na_in_w, j, 0, 3 * d, BF16, **qkv)
            g_l = _proj(hl, na_in_w, j, 3 * d, d, F32)
            if need_ctx:
                qkv_c = _proj(hc, na_in_w, j, 0, 3 * d, BF16, **qkv)
                g_c = _proj(hc, na_in_w, j, 3 * d, d, F32)
                y_l, y_c = _natten(qkv_l, g_l, qkv_c, n_heads, bias, batch, seq, ctx_len, d, g_c)
            else:
                kv_c = _proj(hc, na_in_w, j, d, 2 * d, BF16, **qkv)
                (y_l,) = _natten(qkv_l, g_l, kv_c, 0, bias, batch, seq, ctx_len, d)
                y_c = None
            w_out = na_out_w
        res = dict(mode="resid", mod=mod, mod_layer=i, gate_col0=2 * d)
        xl = _proj(y_l, w_out, j, 0, d, F32, x2=xl, **res, **lat)
        if need_ctx:
            xc = _proj(y_c, w_out, j, 0, d, F32, x2=xc, **res, **con)
    return xl.reshape(batch, seq, d)
```

```python
import functools
import math

import numpy as np
import jax
import jax.numpy as jnp
from jax import lax
from jax.experimental import pallas as pl
from jax.experimental.pallas import tpu as pltpu

GRID_W = 64
NA_KH = 8
NA_KW = 16
HEAD_DIM = 128
LRU_BLOCK = 256
CONV_W = 4
LRU_C = 8.0
EPS = 1e-6
ATTN_SCALE = HEAD_DIM ** -0.5
MASK_VALUE = -1e30

V7X_VMEM_BYTES = 64 * 1024 * 1024
SUBLANES = 8
LANES = 128
MXU_DIM = 256
MOD_ROWS = 8

F32 = jnp.float32
BF16 = jnp.bfloat16


def _vmem_limit(block_bytes):
    return int(min(block_bytes + (16 << 20), V7X_VMEM_BYTES - (4 << 20)))


def _silu(v):
    half = 0.5 * v
    return half * jnp.tanh(half) + half


def _adaln_kernel(c_ref, w_ref, b_ref, o_ref):
    cond = _silu(c_ref[...]).astype(BF16)
    w = w_ref[0].astype(BF16)
    o_ref[0] = jnp.dot(cond, w, preferred_element_type=F32) + b_ref[0]


def _adaln(cvec, ada_w, ada_b):
    n_layers, d, n3 = ada_w.shape
    tn = min(512, n3)
    block_bytes = 2 * d * tn * 4 + d * tn * 2 + 4 * MOD_ROWS * (d + 2 * tn) * 4
    return pl.pallas_call(
        _adaln_kernel,
        out_shape=jax.ShapeDtypeStruct((n_layers, MOD_ROWS, n3), F32),
        grid=(n_layers, n3 // tn),
        in_specs=[
            pl.BlockSpec((MOD_ROWS, d), lambda l, j: (0, 0)),
            pl.BlockSpec((1, d, tn), lambda l, j: (l, 0, j)),
            pl.BlockSpec((1, 1, tn), lambda l, j: (l, 0, j)),
        ],
        out_specs=pl.BlockSpec((1, MOD_ROWS, tn), lambda l, j: (l, 0, j)),
        compiler_params=pltpu.CompilerParams(
            dimension_semantics=("parallel", "parallel"), vmem_limit_bytes=_vmem_limit(block_bytes)),
        name="adaln",
    )(cvec, ada_w, ada_b.reshape(n_layers, 1, n3))


def _norm_kernel(x_ref, g_ref, mod_ref, o_ref, *, tiles_per_group, group0, d):
    grp = group0 + pl.program_id(0) // tiles_per_group
    x = x_ref[...]
    ms = jnp.mean(x * x, axis=-1, keepdims=True)
    y = x * lax.rsqrt(ms + EPS) * g_ref[0]
    shift = mod_ref[0, pl.ds(grp, 1), 0:d]
    scale = mod_ref[0, pl.ds(grp, 1), d:2 * d]
    o_ref[...] = (y * (1.0 + scale) + shift).astype(BF16)


def _norm_mod(x2, norm_g, mod, layer, rows_per_group, group0):
    m, d = x2.shape
    tm = min(256, rows_per_group)
    block_bytes = 2 * tm * d * 4 + 2 * tm * d * 2 + 2 * MOD_ROWS * 3 * d * 4 + 3 * tm * d * 4
    kern = functools.partial(_norm_kernel, tiles_per_group=rows_per_group // tm, group0=group0, d=d)
    return pl.pallas_call(
        kern,
        out_shape=jax.ShapeDtypeStruct((m, d), BF16),
        grid=(m // tm,),
        in_specs=[
            pl.BlockSpec((tm, d), lambda i: (i, 0)),
            pl.BlockSpec((1, 1, d), lambda i: (layer, 0, 0)),
            pl.BlockSpec((1, MOD_ROWS, 3 * d), lambda i: (layer, 0, 0)),
        ],
        out_specs=pl.BlockSpec((tm, d), lambda i: (i, 0)),
        compiler_params=pltpu.CompilerParams(
            dimension_semantics=("parallel",), vmem_limit_bytes=_vmem_limit(block_bytes)),
        name="norm_mod",
    )(x2, norm_g.reshape(norm_g.shape[0], 1, d), mod)


PROJ_ROW_BLOCK = 256


def _proj_tile(a_ref, w, o_ref, extra, row_tile, col_tile, *, mode, tiles_per_group, group0, n_norm_tiles):
    tm = a_ref.shape[0]
    row_block = min(PROJ_ROW_BLOCK, tm)

    def blocks(epilogue):
        for r0 in range(0, tm, row_block):
            rows = slice(r0, r0 + row_block)
            epilogue(rows, jnp.dot(a_ref[rows, :], w, preferred_element_type=F32))

    def plain(rows, acc):
        o_ref[rows, :] = acc.astype(o_ref.dtype)

    if mode == "resid":
        x_ref, gate_ref = extra
        gate = gate_ref[0, pl.ds(group0 + row_tile // tiles_per_group, 1), :]

        def resid(rows, acc):
            o_ref[rows, :] = x_ref[rows, :] + gate * acc

        blocks(resid)
    elif mode == "headnorm":
        (nw_ref,) = extra

        def headnorm(rows, acc):
            for hh in range(acc.shape[1] // HEAD_DIM):
                cols = slice(hh * HEAD_DIM, (hh + 1) * HEAD_DIM)
                blk = acc[:, cols]
                ms = jnp.mean(blk * blk, axis=-1, keepdims=True)
                o_ref[rows, cols] = (blk * lax.rsqrt(ms + EPS) * nw_ref[:, cols]).astype(o_ref.dtype)

        pl.when(col_tile < n_norm_tiles)(lambda: blocks(headnorm))
        pl.when(col_tile >= n_norm_tiles)(lambda: blocks(plain))
    else:
        blocks(plain)


def _proj_kernel(a_ref, w_ref, *rest, j0, **epilogue):
    *extra, o_ref, wb = rest
    j = pl.program_id(0)
    i = pl.program_id(1)

    @pl.when(i == 0)
    def _():
        wb[...] = w_ref[0].astype(BF16)

    _proj_tile(a_ref, wb[...], o_ref, extra, i, j + j0, **epilogue)


def _proj_stream_kernel(a_ref, w_ref, *rest, j0, chunk_rows, **epilogue):
    *extra, o_ref, wb = rest
    jj = pl.program_id(0)
    i = pl.program_id(1)

    def round_chunk():
        r0 = pl.multiple_of(i * chunk_rows, chunk_rows)
        wb[jj % 2, pl.ds(r0, chunk_rows), :] = w_ref[0].astype(BF16)

    @pl.when(jj == 0)
    def _():
        round_chunk()

    @pl.when(jj > 0)
    def _():
        round_chunk()
        _proj_tile(a_ref, wb[(jj - 1) % 2], o_ref, extra, i, jj - 1 + j0, **epilogue)


def _proj(a, w_stack, layer, col0, n_out, out_dtype, *, mode="plain", norm_w=None, n_norm_cols=0,
          x2=None, mod=None, mod_layer=0, gate_col0=0, rows_per_group=None, group0=0):
    m, k = a.shape
    rows_per_group = rows_per_group or m
    tm = min(1024, rows_per_group)
    n_row_tiles = m // tm
    stream = n_row_tiles > 1
    tn = 512 if mode == "resid" or not stream else 1024
    while any(extent % tn for extent in (n_out, col0, n_norm_cols, gate_col0)):
        tn //= 2
    assert tn >= LANES
    n_col_tiles = n_out // tn
    j0 = col0 // tn
    out_bytes = jnp.dtype(out_dtype).itemsize
    block_bytes = 2 * tm * k * 2 + 2 * tm * tn * out_bytes + tm * tn * 4
    epilogue = dict(mode=mode, tiles_per_group=rows_per_group // tm, group0=group0,
                    n_norm_tiles=n_norm_cols // tn)
    if stream:
        assert k % n_row_tiles == 0
        chunk_rows = k // n_row_tiles
        block_bytes += 2 * chunk_rows * tn * 4 + 2 * k * tn * 2
        grid = (n_col_tiles + 1, n_row_tiles)

        def row_of(jj, i):
            return jnp.where(jj == 0, 0, i)

        def col_of(jj):
            return jnp.maximum(jj - 1, 0)

        in_specs = [
            pl.BlockSpec((tm, k), lambda jj, i: (row_of(jj, i), 0)),
            pl.BlockSpec((1, chunk_rows, tn), lambda jj, i: (
                layer, jnp.where(jj == n_col_tiles, 0, i), j0 + jnp.minimum(jj, n_col_tiles - 1))),
        ]
        scratch = pltpu.VMEM((2, k, tn), BF16)
        kern = functools.partial(_proj_stream_kernel, j0=j0, chunk_rows=chunk_rows, **epilogue)
    else:
        block_bytes += 2 * k * tn * 4 + k * tn * 2
        grid = (n_col_tiles, n_row_tiles)

        def row_of(jj, i):
            return i

        def col_of(jj):
            return jj

        in_specs = [
            pl.BlockSpec((tm, k), lambda jj, i: (i, 0)),
            pl.BlockSpec((1, k, tn), lambda jj, i: (layer, 0, j0 + jj)),
        ]
        scratch = pltpu.VMEM((k, tn), BF16)
        kern = functools.partial(_proj_kernel, j0=j0, **epilogue)
    args = [a, w_stack]
    if mode == "headnorm":
        in_specs.append(pl.BlockSpec((1, tn), lambda jj, i: (0, j0 + col_of(jj))))
        args.append(norm_w)
    elif mode == "resid":
        g0 = gate_col0 // tn
        in_specs += [pl.BlockSpec((tm, tn), lambda jj, i: (row_of(jj, i), col_of(jj))),
                     pl.BlockSpec((1, MOD_ROWS, tn), lambda jj, i: (mod_layer, 0, g0 + col_of(jj)))]
        args += [x2, mod]
        block_bytes += 2 * tm * tn * 4
    return pl.pallas_call(
        kern,
        out_shape=jax.ShapeDtypeStruct((m, n_out), out_dtype),
        grid=grid,
        in_specs=in_specs,
        out_specs=pl.BlockSpec((tm, tn), lambda jj, i: (row_of(jj, i), col_of(jj))),
        scratch_shapes=[scratch],
        compiler_params=pltpu.CompilerParams(
            dimension_semantics=("arbitrary", "arbitrary"), vmem_limit_bytes=_vmem_limit(block_bytes)),
        name="proj_" + mode,
    )(*args)


CONV_LEFT = CONV_W // 2
N_SEGMENTS = SUBLANES
SCAN_BLOCK = 8


def _scan_rows(c, tc, seg_len):
    if seg_len >= tc:
        chunks_per_seg = seg_len // tc
        s = lax.div(c, chunks_per_seg)
        k0 = lax.rem(c, chunks_per_seg) * tc
        return [(0, tc, s + N_SEGMENTS * k0, s)]
    segs_per_chunk = tc // seg_len
    return [(e * seg_len, seg_len, c * segs_per_chunk + e, c * segs_per_chunk + e) for e in range(segs_per_chunk)]


def _lru_to_scan_order(src_ref, us_ref, length, tc):
    n_slabs = us_ref.shape[0]
    seg_len = length // N_SEGMENTS
    lead = CONV_LEFT * N_SEGMENTS

    def body(c, carry):
        base = pl.multiple_of(c * tc, tc)
        for row0, n_rows, scan_row0, _ in _scan_rows(c, tc, seg_len):
            dst = pl.ds(lead + scan_row0, n_rows, stride=N_SEGMENTS)
            for slab in range(n_slabs):
                us_ref[slab, dst, :] = src_ref[pl.ds(base + row0, n_rows), slab * LANES:(slab + 1) * LANES]
        return carry

    lax.fori_loop(0, length // tc, body, 0)

    seg = lax.broadcasted_iota(jnp.int32, (N_SEGMENTS, LANES), 0)
    for slab in range(n_slabs):
        def step(k):
            return us_ref[slab, pl.ds(lead + k * N_SEGMENTS, N_SEGMENTS), :]

        for back in range(1, CONV_LEFT + 1):
            halo = jnp.where(seg >= 1, pltpu.roll(step(seg_len - back), 1, 0), 0.0)
            us_ref[slab, pl.ds(lead - back * N_SEGMENTS, N_SEGMENTS), :] = halo
        for ahead in range(CONV_W - 1 - CONV_LEFT):
            halo = jnp.where(seg < N_SEGMENTS - 1, pltpu.roll(step(ahead), N_SEGMENTS - 1, 0), 0.0)
            us_ref[slab, pl.ds(lead + (seg_len + ahead) * N_SEGMENTS, N_SEGMENTS), :] = halo


def _lru_coefficients(us_ref, co_ref, cw_ref, cb_ref, gw_ref, gb_ref, cvec_log2, length, tc):
    n_slabs = us_ref.shape[0]
    width = n_slabs * LANES

    def body(c, carry):
        base = pl.multiple_of(c * tc, tc)
        win = jnp.concatenate([us_ref[slab, pl.ds(base, tc + (CONV_W - 1) * N_SEGMENTS), :]
                               for slab in range(n_slabs)], axis=1)
        u = cw_ref[0:1, :] * win[0:tc]
        for kk in range(1, CONV_W):
            u = u + cw_ref[kk:kk + 1, :] * win[kk * N_SEGMENTS:kk * N_SEGMENTS + tc]
        u = u + cb_ref[...]
        zh = jnp.dot(u.astype(BF16), gw_ref[0], preferred_element_type=F32) + gb_ref[0]
        u_half = 0.5 * u
        c_half = 0.5 * cvec_log2
        for d in range(2):
            th_r = jnp.tanh(zh[:, 2 * d * width:(2 * d + 1) * width])
            th_i = jnp.tanh(zh[:, (2 * d + 1) * width:(2 * d + 2) * width])
            a = jnp.exp2(th_r * c_half[d:d + 1] + c_half[d:d + 1])
            gated_u = th_i * u_half + u_half
            t = 1.0 - a * a
            b = jnp.where(t > 0.0, t * lax.rsqrt(t), 0.0) * gated_u
            for slab in range(n_slabs):
                lanes = slice(slab * LANES, (slab + 1) * LANES)
                co_ref[2 * d, slab, pl.ds(base, tc), :] = a[:, lanes]
                co_ref[2 * d + 1, slab, pl.ds(base, tc), :] = b[:, lanes]
        return carry

    lax.fori_loop(0, length // tc, body, 0)


def _lru_local_scan(co_ref, length):
    n_slabs = co_ref.shape[1]
    n_blocks = length // N_SEGMENTS // SCAN_BLOCK
    block_rows = SCAN_BLOCK * N_SEGMENTS
    zero = jnp.zeros((N_SEGMENTS, LANES), F32)
    one = jnp.ones((N_SEGMENTS, LANES), F32)

    def body(i, state):
        rows = (pl.ds(pl.multiple_of(i * block_rows, block_rows), block_rows),
                pl.ds(pl.multiple_of((n_blocks - 1 - i) * block_rows, block_rows), block_rows))
        new_state = []
        for d in range(2):
            for slab in range(n_slabs):
                h, p = state[d * n_slabs + slab]
                a_blk = co_ref[2 * d, slab, rows[d], :]
                b_blk = co_ref[2 * d + 1, slab, rows[d], :]
                hs, ps = [None] * SCAN_BLOCK, [None] * SCAN_BLOCK
                for j in (range(SCAN_BLOCK) if d == 0 else range(SCAN_BLOCK - 1, -1, -1)):
                    step = slice(j * N_SEGMENTS, (j + 1) * N_SEGMENTS)
                    h = a_blk[step] * h + b_blk[step]
                    p = a_blk[step] * p
                    hs[j], ps[j] = h, p
                co_ref[2 * d, slab, rows[d], :] = jnp.concatenate(ps, axis=0)
                co_ref[2 * d + 1, slab, rows[d], :] = jnp.concatenate(hs, axis=0)
                new_state.append((h, p))
        return tuple(new_state)

    lax.fori_loop(0, n_blocks, body, tuple((zero, one) for _ in range(2 * n_slabs)))


def _lru_carries(co_ref, carry_ref, length, h0_fwd, h0_bwd):
    n_slabs = co_ref.shape[1]
    seg_len = length // N_SEGMENTS
    ends = (pl.ds((seg_len - 1) * N_SEGMENTS, N_SEGMENTS), pl.ds(0, N_SEGMENTS))
    finals = []
    for d, h0 in enumerate((h0_fwd, h0_bwd)):
        p_end = jnp.concatenate([co_ref[2 * d, slab, ends[d], :] for slab in range(n_slabs)], axis=1)
        h_end = jnp.concatenate([co_ref[2 * d + 1, slab, ends[d], :] for slab in range(n_slabs)], axis=1)
        carry = h0
        for s in (range(N_SEGMENTS) if d == 0 else range(N_SEGMENTS - 1, -1, -1)):
            carry_ref[d, pl.ds(s, 1), :] = carry
            carry = h_end[s:s + 1] + p_end[s:s + 1] * carry
        finals.append(carry)
    return finals


def _lru_output(co_ref, carry_ref, gate_ref, out_ref, length, tc):
    n_slabs = co_ref.shape[1]
    seg_len = length // N_SEGMENTS
    steps = tc // N_SEGMENTS

    def combine(c, carry):
        rows = pl.ds(pl.multiple_of(c * tc, tc), tc)
        for slab in range(n_slabs):
            lanes = slice(slab * LANES, (slab + 1) * LANES)
            hsum = None
            for d in range(2):
                cin = carry_ref[d, :, lanes][None]
                p = co_ref[2 * d, slab, rows, :].reshape(steps, N_SEGMENTS, LANES)
                h = co_ref[2 * d + 1, slab, rows, :].reshape(steps, N_SEGMENTS, LANES) + p * cin
                hsum = h if hsum is None else hsum + h
            co_ref[1, slab, rows, :] = hsum.reshape(tc, LANES)
        return carry

    lax.fori_loop(0, length // tc, combine, 0)

    def emit(c, carry):
        base = pl.multiple_of(c * tc, tc)
        for row0, n_rows, scan_row0, _ in _scan_rows(c, tc, seg_len):
            src = pl.ds(scan_row0, n_rows, stride=N_SEGMENTS)
            hsum = jnp.concatenate([co_ref[1, slab, src, :] for slab in range(n_slabs)], axis=1)
            rows = pl.ds(base + row0, n_rows)
            out_ref[rows, :] = (hsum * _silu(gate_ref[rows, :])).astype(out_ref.dtype)
        return carry

    lax.fori_loop(0, length // tc, emit, 0)


def _rglru_kernel(u_ref, g_ref, uc_ref, gc_ref, cw_ref, cb_ref, gw_ref, gb_ref, lam_ref,
                  y_ref, yc_ref, us_l, us_c, co_l, co_c, carry_l, carry_c, *, seq, ctx_len, tc):
    width = u_ref.shape[1]
    lam = lam_ref[0]
    softplus_neg = jnp.maximum(-lam, 0.0) + jnp.log1p(jnp.exp(-jnp.abs(lam)))
    cvec_log2 = (-LRU_C * math.log2(math.e)) * softplus_neg
    params = (cw_ref.at[0], cb_ref.at[0], gw_ref, gb_ref, cvec_log2)

    zeros = jnp.zeros((1, width), F32)
    _lru_to_scan_order(uc_ref, us_c, ctx_len, tc)
    _lru_coefficients(us_c, co_c, *params, ctx_len, tc)
    _lru_local_scan(co_c, ctx_len)
    h_fwd, h_bwd = _lru_carries(co_c, carry_c, ctx_len, zeros, zeros)
    _lru_output(co_c, carry_c, gc_ref, yc_ref, ctx_len, tc)

    _lru_to_scan_order(u_ref, us_l, seq, tc)
    _lru_coefficients(us_l, co_l, *params, seq, tc)
    _lru_local_scan(co_l, seq)
    _lru_carries(co_l, carry_l, seq, h_fwd, h_bwd)
    _lru_output(co_l, carry_l, g_ref, y_ref, seq, tc)


def _rglru(ug_l, ug_c, conv_w, conv_b, gate_w, gate_b, lam, layer, batch, seq, ctx_len):
    d = conv_w.shape[2]
    nb = d // LRU_BLOCK
    w = LRU_BLOCK
    tc = min(256, ctx_len, seq)
    for length in (seq, ctx_len):
        seg_len = length // N_SEGMENTS
        assert length % (N_SEGMENTS * SCAN_BLOCK) == 0 and length % tc == 0
        assert seg_len % tc == 0 or tc % seg_len == 0
    n_slabs = w // LANES
    halo_rows = (CONV_W - 1) * N_SEGMENTS
    gw = (0.5 * jnp.transpose(gate_w[layer], (2, 3, 0, 1, 4))).reshape(nb, w, 4 * w).astype(BF16)
    gb = 0.5 * jnp.transpose(gate_b[layer].reshape(2, 2, nb, w), (2, 0, 1, 3)).reshape(nb, 1, 4 * w)
    block_bytes = (2 * 2 * (seq + ctx_len) * w * 4 + 2 * (seq + ctx_len) * w * 2 + 2 * w * 4 * w * 2
                   + (5 * (seq + ctx_len) + 2 * halo_rows) * w * 4 + 12 * tc * w * 4)
    kern = functools.partial(_rglru_kernel, seq=seq, ctx_len=ctx_len, tc=tc)
    return pl.pallas_call(
        kern,
        out_shape=(jax.ShapeDtypeStruct((batch * seq, d), BF16),
                   jax.ShapeDtypeStruct((batch * ctx_len, d), BF16)),
        grid=(batch, nb),
        in_specs=[
            pl.BlockSpec((seq, w), lambda b, n: (b, n)),
            pl.BlockSpec((seq, w), lambda b, n: (b, nb + n)),
            pl.BlockSpec((ctx_len, w), lambda b, n: (b, n)),
            pl.BlockSpec((ctx_len, w), lambda b, n: (b, nb + n)),
            pl.BlockSpec((1, CONV_W, w), lambda b, n: (layer, 0, n)),
            pl.BlockSpec((1, 1, w), lambda b, n: (layer, 0, n)),
            pl.BlockSpec((1, w, 4 * w), lambda b, n: (n, 0, 0)),
            pl.BlockSpec((1, 1, 4 * w), lambda b, n: (n, 0, 0)),
            pl.BlockSpec((1, 2, w), lambda b, n: (layer, 0, n)),
        ],
        out_specs=(pl.BlockSpec((seq, w), lambda b, n: (b, n)),
                   pl.BlockSpec((ctx_len, w), lambda b, n: (b, n))),
        scratch_shapes=[
            pltpu.VMEM((n_slabs, seq + halo_rows, LANES), F32),
            pltpu.VMEM((n_slabs, ctx_len + halo_rows, LANES), F32),
            pltpu.VMEM((4, n_slabs, seq, LANES), F32),
            pltpu.VMEM((4, n_slabs, ctx_len, LANES), F32),
            pltpu.VMEM((2, N_SEGMENTS, w), F32),
            pltpu.VMEM((2, N_SEGMENTS, w), F32),
        ],
        compiler_params=pltpu.CompilerParams(
            dimension_semantics=("parallel", "parallel"), vmem_limit_bytes=_vmem_limit(block_bytes)),
        name="rglru",
    )(ug_l, ug_l, ug_c, ug_c, conv_w, conv_b, gw, gb, lam)


NA_QROWS = 4
NA_WIN_ROWS = 12
assert NA_WIN_ROWS >= NA_QROWS + NA_KH - 1 and (NA_WIN_ROWS * GRID_W) % MXU_DIM == 0
NA_BLOCKS_PER_STEP = 14
NA_TILE_ROWS = LANES // GRID_W
assert NA_TILE_ROWS == 2 and NA_WIN_ROWS % NA_TILE_ROWS == 0
N_BIAS_ROWS = 2 * NA_KH - 1
BIAS_BLOCKS = N_BIAS_ROWS + 3


def _nt_dot(a, b):
    return lax.dot_general(a, b, (((1,), (1,)), ((), ())), preferred_element_type=F32)


def _softmax_pv(scores, values):
    m = scores[0].max(axis=-1, keepdims=True)
    for s in scores[1:]:
        m = jnp.maximum(m, s.max(axis=-1, keepdims=True))
    ps = [jnp.exp(s - m) for s in scores]
    denom = ps[0].sum(axis=-1, keepdims=True)
    for p in ps[1:]:
        denom = denom + p.sum(axis=-1, keepdims=True)
    o = jnp.dot(ps[0].astype(BF16), values[0], preferred_element_type=F32)
    for p, v in zip(ps[1:], values[1:]):
        o = o + jnp.dot(p.astype(BF16), v, preferred_element_type=F32)
    return o / denom


def _tile_plan(rows):
    n_blocks = rows // NA_QROWS
    plans = {}
    for blk in range(n_blocks):
        kind = "first" if blk == 0 else ("last" if blk == n_blocks - 1 else "inner")
        win_start = min(max(blk * NA_QROWS - NA_KH // 2, 0), rows - NA_WIN_ROWS)
        plan = []
        for qi in range(NA_QROWS):
            r = blk * NA_QROWS + qi
            row_start = min(max(r - NA_KH // 2, 0), rows - NA_KH)
            assert win_start <= row_start and row_start + NA_KH <= win_start + NA_WIN_ROWS
            tiles = []
            for t in range(NA_WIN_ROWS // NA_TILE_ROWS):
                key_rows = [win_start + t * NA_TILE_ROWS + e for e in range(NA_TILE_ROWS)]
                inside = [row_start <= kr < row_start + NA_KH for kr in key_rows]
                if not any(inside):
                    tiles.append(None)
                    continue
                e0 = inside.index(True)
                block = 1 + (key_rows[e0] - r + NA_KH - 1) - e0
                assert 0 <= block and block + NA_TILE_ROWS <= BIAS_BLOCKS - 1
                keep = "both" if all(inside) else ("left" if inside[0] else "right")
                tiles.append((block % 2, (block - block % 2) * GRID_W, keep))
            plan.append(tuple(tiles))
        assert plans.setdefault(kind, tuple(plan)) == tuple(plan)
    return plans


def _attend_block(q, kw, vw, kc, vc, bias_ref, plan):
    s_lat = _nt_dot(q, kw)
    s_ctx = _nt_dot(q, kc)
    lane = lax.broadcasted_iota(jnp.int32, (GRID_W, LANES), 1)
    p_lat_rows, p_ctx_rows, denoms = [], [], []
    for qi, tiles in enumerate(plan):
        rs = slice(qi * GRID_W, (qi + 1) * GRID_W)
        scored = []
        for t, spec in enumerate(tiles):
            if spec is None:
                scored.append(None)
                continue
            copy, off, keep = spec
            s = s_lat[rs, t * LANES:(t + 1) * LANES] + bias_ref[0, copy, :, off:off + LANES]
            if keep == "left":
                s = jnp.where(lane < GRID_W, s, MASK_VALUE)
            elif keep == "right":
                s = jnp.where(lane >= GRID_W, s, MASK_VALUE)
            scored.append(s)
        ctx_tiles = [s_ctx[rs, t * LANES:(t + 1) * LANES] for t in range(s_ctx.shape[1] // LANES)]
        live = [s for s in scored if s is not None] + ctx_tiles
        m = functools.reduce(jnp.maximum, live).max(axis=-1, keepdims=True)
        p_lat = [None if s is None else jnp.exp(s - m) for s in scored]
        p_ctx = [jnp.exp(s - m) for s in ctx_tiles]
        total = functools.reduce(jnp.add, [p for p in p_lat if p is not None] + p_ctx)
        denoms.append(total.sum(axis=-1, keepdims=True))
        zero = jnp.zeros((GRID_W, LANES), BF16)
        p_lat_rows.append(jnp.concatenate([zero if p is None else p.astype(BF16) for p in p_lat], axis=1))
        p_ctx_rows.append(jnp.concatenate([p.astype(BF16) for p in p_ctx], axis=1))
    o = jnp.dot(jnp.concatenate(p_lat_rows, axis=0), vw, preferred_element_type=F32)
    o = o + jnp.dot(jnp.concatenate(p_ctx_rows, axis=0), vc, preferred_element_type=F32)
    return o / jnp.concatenate(denoms, axis=0)


def _natten_kernel(*refs, rows, need_ctx):
    if need_ctx:
        q_ref, k_ref, v_ref, g_ref, kc_ref, vc_ref, bias_ref, qc_ref, gc_ref, y_ref, yc_ref = refs
    else:
        q_ref, k_ref, v_ref, g_ref, kc_ref, vc_ref, bias_ref, y_ref = refs
    kc = kc_ref[...]
    vc = vc_ref[...]
    n_blocks = rows // NA_QROWS
    nq = NA_QROWS * GRID_W
    nk = NA_WIN_ROWS * GRID_W
    plans = _tile_plan(rows)

    def attend(blk, kind):
        if isinstance(blk, int):
            q0 = blk * nq
            k0 = min(max(blk * NA_QROWS - NA_KH // 2, 0), rows - NA_WIN_ROWS) * GRID_W
        else:
            q0 = pl.multiple_of(blk * nq, nq)
            k0 = pl.multiple_of(jnp.clip(blk * NA_QROWS - NA_KH // 2, 0, rows - NA_WIN_ROWS) * GRID_W, GRID_W)
        o = _attend_block(q_ref[pl.ds(q0, nq), :], k_ref[pl.ds(k0, nk), :], v_ref[pl.ds(k0, nk), :],
                          kc, vc, bias_ref, plans[kind])
        y_ref[pl.ds(q0, nq), :] = (o * _silu(g_ref[pl.ds(q0, nq), :])).astype(y_ref.dtype)

    def inner(it, carry):
        for sub in range(NA_BLOCKS_PER_STEP):
            attend(1 + it * NA_BLOCKS_PER_STEP + sub, "inner")
        return carry

    attend(0, "first")
    lax.fori_loop(0, (n_blocks - 2) // NA_BLOCKS_PER_STEP, inner, 0)
    attend(n_blocks - 1, "last")

    if need_ctx:
        oc = _softmax_pv([_nt_dot(qc_ref[...], kc)], [vc])
        yc_ref[...] = (oc * _silu(gc_ref[...])).astype(yc_ref.dtype)


def _bias_table(rpb):
    n_heads = rpb.shape[0]
    qc = np.arange(GRID_W)[:, None]
    kcol = np.arange(GRID_W)[None, :]
    col_start = np.clip(qc - NA_KW // 2, 0, GRID_W - NA_KW)
    inside = (kcol >= col_start) & (kcol < col_start + NA_KW)
    col_idx = np.clip(kcol - qc, -(NA_KW - 1), NA_KW - 1) + (NA_KW - 1)
    onehot = (col_idx[:, :, None] == np.arange(2 * NA_KW - 1)) & inside[:, :, None]
    tab = jnp.einsum('ckm,hjm->hcjk', jnp.asarray(onehot, F32), rpb, precision=lax.Precision.HIGHEST)
    tab = tab + jnp.asarray(np.where(inside, 0.0, MASK_VALUE), F32)[None, :, None, :]
    tab = tab.reshape(n_heads, GRID_W, N_BIAS_ROWS * GRID_W)

    def padded(left_blocks):
        right_blocks = BIAS_BLOCKS - N_BIAS_ROWS - left_blocks
        return jnp.pad(tab, ((0, 0), (0, 0), (left_blocks * GRID_W, right_blocks * GRID_W)),
                       constant_values=MASK_VALUE)

    return jnp.stack([padded(1), padded(0)], axis=1)


def _natten(qkv_l, g_l, kv_c, kv_c_head0, bias, batch, seq, ctx_len, d, g_c=None):
    n_heads = d // HEAD_DIM
    rows = seq // GRID_W
    need_ctx = g_c is not None
    hd = HEAD_DIM
    in_specs = [
        pl.BlockSpec((seq, hd), lambda b, h: (b, h)),
        pl.BlockSpec((seq, hd), lambda b, h: (b, n_heads + h)),
        pl.BlockSpec((seq, hd), lambda b, h: (b, 2 * n_heads + h)),
        pl.BlockSpec((seq, hd), lambda b, h: (b, h)),
        pl.BlockSpec((ctx_len, hd), lambda b, h: (b, kv_c_head0 + h)),
        pl.BlockSpec((ctx_len, hd), lambda b, h: (b, kv_c_head0 + n_heads + h)),
        pl.BlockSpec((1,) + bias.shape[1:], lambda b, h: (h, 0, 0, 0)),
    ]
    args = [qkv_l, qkv_l, qkv_l, g_l, kv_c, kv_c, bias]
    out_shape = [jax.ShapeDtypeStruct((batch * seq, d), BF16)]
    out_specs = [pl.BlockSpec((seq, hd), lambda b, h: (b, h))]
    if need_ctx:
        in_specs += [pl.BlockSpec((ctx_len, hd), lambda b, h: (b, h)),
                     pl.BlockSpec((ctx_len, hd), lambda b, h: (b, h))]
        args += [kv_c, g_c]
        out_shape.append(jax.ShapeDtypeStruct((batch * ctx_len, d), BF16))
        out_specs.append(pl.BlockSpec((ctx_len, hd), lambda b, h: (b, h)))
    nq, nk = NA_QROWS * GRID_W, NA_WIN_ROWS * GRID_W
    block_bytes = (2 * (4 * seq * hd * 2 + seq * hd * 4 + 4 * ctx_len * hd * 4)
                   + 2 * bias[0].size * 4 + 4 * NA_BLOCKS_PER_STEP * nq * (nk + ctx_len) * 4)
    kern = functools.partial(_natten_kernel, rows=rows, need_ctx=need_ctx)
    return pl.pallas_call(
        kern,
        out_shape=tuple(out_shape),
        grid=(batch, n_heads),
        in_specs=in_specs,
        out_specs=tuple(out_specs),
        compiler_params=pltpu.CompilerParams(
            dimension_semantics=("parallel", "parallel"), vmem_limit_bytes=_vmem_limit(block_bytes)),
        name="natten",
    )(*args)


def kernel(x, c, ctx, c_ctx, ada_w, ada_b, norm_g, lru_in_w, lru_conv_w, lru_conv_b, lru_gate_w,
           lru_gate_b, lru_lambda, lru_out_w, na_in_w, na_qk_norm, na_rpb, na_out_w):
    batch, seq, d = x.shape
    ctx_len = ctx.shape[1]
    depth = ada_w.shape[0]
    n_heads = d // HEAD_DIM
    rows = seq // GRID_W
    assert batch + 1 <= MOD_ROWS and seq % GRID_W == 0
    assert rows % NA_QROWS == 0 and rows >= NA_WIN_ROWS + NA_QROWS
    assert (rows // NA_QROWS - 2) % NA_BLOCKS_PER_STEP == 0 and ctx_len % LANES == 0

    xl = x.reshape(batch * seq, d)
    xc = ctx.reshape(batch * ctx_len, d)
    cvec = jnp.zeros((MOD_ROWS, d), F32).at[:batch].set(c).at[batch].set(c_ctx)
    mod = _adaln(cvec, ada_w, ada_b)
    lat = dict(rows_per_group=seq, group0=0)
    con = dict(rows_per_group=batch * ctx_len, group0=batch)
    conv_b = lru_conv_b.reshape(lru_conv_b.shape[0], 1, d)

    for i in range(depth):
        need_ctx = i < depth - 1
        j = i // 2
        hl = _norm_mod(xl, norm_g, mod, i, **lat)
        hc = _norm_mod(xc, norm_g, mod, i, **con)
        if i % 2 == 0:
            assert need_ctx
            ug_l = _proj(hl, lru_in_w, j, 0, 2 * d, F32)
            ug_c = _proj(hc, lru_in_w, j, 0, 2 * d, F32)
            y_l, y_c = _rglru(ug_l, ug_c, lru_conv_w, conv_b, lru_gate_w, lru_gate_b, lru_lambda,
                              j, batch, seq, ctx_len)
            w_out = lru_out_w
        else:
            norm_w = jnp.concatenate([jnp.tile(na_qk_norm[j, 0] * ATTN_SCALE, n_heads),
                                      jnp.tile(na_qk_norm[j, 1], n_heads),
                                      jnp.ones((d,), F32)]).reshape(1, 3 * d)
            bias = _bias_table(na_rpb[j])
            qkv = dict(mode="headnorm", norm_w=norm_w, n_norm_cols=2 * d)
            qkv_l = _proj(hl, na_in_w, j, 0, 3 * d, BF16, **qkv)
            g_l = _proj(hl, na_in_w, j, 3 * d, d, F32)
            if need_ctx:
                qkv_c = _proj(hc, na_in_w, j, 0, 3 * d, BF16, **qkv)
                g_c = _proj(hc, na_in_w, j, 3 * d, d, F32)
                y_l, y_c = _natten(qkv_l, g_l, qkv_c, n_heads, bias, batch, seq, ctx_len, d, g_c)
            else:
                kv_c = _proj(hc, na_in_w, j, d, 2 * d, BF16, **qkv)
                (y_l,) = _natten(qkv_l, g_l, kv_c, 0, bias, batch, seq, ctx_len, d)
                y_c = None
            w_out = na_out_w
        res = dict(mode="resid", mod=mod, mod_layer=i, gate_col0=2 * d)
        xl = _proj(y_l, w_out, j, 0, d, F32, x2=xl, **res, **lat)
        if need_ctx:
            xc = _proj(y_c, w_out, j, 0, d, F32, x2=xc, **res, **con)
    return xl.reshape(batch, seq, d)
```
